```python
import jax
import jax.numpy as jnp
from jax import lax
import numpy as np

D_MODEL = 1024
BATCH = 8
SEQ = 8192
DEPTH = 2

GRID_W = 64
CTX_LEN = 256
N_EVEN = (DEPTH + 1) // 2
N_ODD = DEPTH // 2

HEAD_DIM = 64
ATTN_HEADS = (D_MODEL // 2) // HEAD_DIM
ATTN_KV_HEADS = ATTN_HEADS // 4
ATTN_GROUP = ATTN_HEADS // ATTN_KV_HEADS
WINDOW = 128
ATTN_BLOCK = 128
ROPE_BASE = 10000.0

HGRN_EXPAND = 128
HGRN_WIDTH = D_MODEL // 2
HGRN_HEADS = HGRN_WIDTH // HGRN_EXPAND
HGRN_DK = HGRN_EXPAND
HGRN_DV = HGRN_EXPAND
CHUNK = 64

RET_HEADS = 4
RET_DK = D_MODEL // RET_HEADS
RET_DV = 2 * RET_DK
RET_BASE = 10000.0

FFN_HIDDEN = -(-8 * D_MODEL // (3 * 256)) * 256

A_Q = ATTN_HEADS * HEAD_DIM
A_KV = ATTN_KV_HEADS * HEAD_DIM
EVEN_PARTS = (('a_q', A_Q), ('a_k', A_KV), ('a_v', A_KV), ('b_q', HGRN_WIDTH), ('b_ff', HGRN_WIDTH),
              ('b_fb', HGRN_WIDTH), ('b_i', HGRN_WIDTH), ('b_g', HGRN_WIDTH))
EVEN_IN = A_Q + 2 * A_KV + 5 * HGRN_WIDTH
EVEN_OUT = A_Q + HGRN_HEADS * HGRN_DV
ODD_PARTS = (('q', RET_HEADS * RET_DK), ('k', RET_HEADS * RET_DK), ('v', RET_HEADS * RET_DV),
             ('g', RET_HEADS * RET_DV))
ODD_IN = 2 * RET_HEADS * RET_DK + 2 * RET_HEADS * RET_DV
ODD_OUT = RET_HEADS * RET_DV
EPS = 1e-6
F32 = jnp.float32

kernel_name = 'hybrid_swa_hgrn2_retention_dit_prefix'


def _offsets(parts):
    out, start = {}, 0
    for name, width in parts:
        out[name] = (start, start + width)
        start += width
    return out


def split_parts(u, parts):
    return {n: u[..., s:e] for n, (s, e) in _offsets(parts).items()}


def project_parts(h, w, parts, names):
    off = _offsets(parts)
    return {n: h @ w[:, off[n][0]:off[n][1]] for n in names}


def rms_norm(x, g=None):
    xf = x.astype(F32)
    y = xf * lax.rsqrt(jnp.mean(xf * xf, axis=-1, keepdims=True) + EPS)
    if g is not None:
        y = y * g.astype(F32)
    return y.astype(x.dtype)


def modulate(h, shift, scale):
    return h * (1.0 + scale) + shift


def to_heads(u, n_heads):
    b, l, _ = u.shape
    return u.reshape(b, l, n_heads, -1).transpose(0, 2, 1, 3)


def from_heads(o):
    return o.transpose(0, 2, 1, 3)


def flip(a):
    return jnp.flip(a, axis=2)


def axial_rope_tables(n_tok):
    t = jnp.arange(n_tok)
    row = (t // GRID_W).astype(F32)
    col = (t % GRID_W).astype(F32)
    n_freq = HEAD_DIM // 4
    inv = ROPE_BASE ** (-jnp.arange(n_freq, dtype=F32) / n_freq)
    ang = jnp.concatenate([row[:, None] * inv, col[:, None] * inv], axis=-1)
    return jnp.cos(ang), jnp.sin(ang)


def retention_rope_tables(n_tok):
    theta = 1.0 / (RET_BASE ** jnp.linspace(0.0, 1.0, RET_DK // 2, dtype=F32))
    ang = jnp.arange(n_tok, dtype=F32)[:, None] * theta
    return jnp.cos(ang), jnp.sin(ang)


def apply_rope(x, cos, sin):
    half = x.shape[-1] // 2
    xf = x.astype(F32)
    x1, x2 = xf[..., :half], xf[..., half:]
    return jnp.concatenate([x1 * cos - x2 * sin, x2 * cos + x1 * sin], axis=-1).astype(x.dtype)


def attn_q_heads(u, g):
    b, l, _ = u.shape
    q = rms_norm(u.reshape(b, l, ATTN_KV_HEADS, ATTN_GROUP, HEAD_DIM), g)
    return q.transpose(0, 2, 3, 1, 4)


def kv_heads(u, g=None):
    b, l, _ = u.shape
    h = u.reshape(b, l, ATTN_KV_HEADS, HEAD_DIM)
    if g is not None:
        h = rms_norm(h, g)
    return h.transpose(0, 2, 1, 3)


def window_attention(q, k, v, k_ctx, v_ctx, sink):
    b, kv, g, l, d = q.shape
    n_blocks = l // ATTN_BLOCK
    span = ATTN_BLOCK + 2 * WINDOW
    pad = ((0, 0), (0, 0), (WINDOW, WINDOW), (0, 0))
    kp, vp = jnp.pad(k, pad), jnp.pad(v, pad)
    scale = d ** -0.5
    sink_col = jnp.broadcast_to(sink[None, :, :, None, None], (b, kv, g, ATTN_BLOCK, 1))

    def block(i):
        start = i * ATTN_BLOCK
        qb = lax.dynamic_slice_in_dim(q, start, ATTN_BLOCK, axis=3)
        kb = lax.dynamic_slice_in_dim(kp, start, span, axis=2)
        vb = lax.dynamic_slice_in_dim(vp, start, span, axis=2)
        q_pos = start + jnp.arange(ATTN_BLOCK)
        k_pos = start - WINDOW + jnp.arange(span)
        valid = ((jnp.abs(k_pos[None, :] - q_pos[:, None]) <= WINDOW)
                 & (k_pos >= 0)[None, :] & (k_pos < l)[None, :])
        s_lat = jnp.einsum('bkgqd,bksd->bkgqs', qb, kb, preferred_element_type=F32) * scale
        s_lat = jnp.where(valid, s_lat, -jnp.inf)
        s_ctx = jnp.einsum('bkgqd,bkcd->bkgqc', qb, k_ctx, preferred_element_type=F32) * scale
        p = jax.nn.softmax(jnp.concatenate([sink_col, s_lat, s_ctx], axis=-1), axis=-1)
        p_lat = p[..., 1:1 + span].astype(v.dtype)
        p_ctx = p[..., 1 + span:].astype(v.dtype)
        return (jnp.einsum('bkgqs,bksd->bkgqd', p_lat, vb)
                + jnp.einsum('bkgqc,bkcd->bkgqd', p_ctx, v_ctx))

    o = lax.map(block, jnp.arange(n_blocks))
    return o.transpose(1, 0, 4, 2, 3, 5).reshape(b, l, kv * g * d)


def context_attention(q, k, v, sink):
    b, kv, g, lc, d = q.shape
    s = jnp.einsum('bkgqd,bkcd->bkgqc', q, k, preferred_element_type=F32) * d ** -0.5
    sink_col = jnp.broadcast_to(sink[None, :, :, None, None], (b, kv, g, lc, 1))
    p = jax.nn.softmax(jnp.concatenate([sink_col, s], axis=-1), axis=-1)[..., 1:].astype(v.dtype)
    o = jnp.einsum('bkgqc,bkcd->bkgqd', p, v)
    return o.transpose(0, 3, 1, 2, 4).reshape(b, lc, kv * g * d)


def chunk_scan(q_in, k_in, v, decay, s0):
    xs = (jnp.moveaxis(q_in, 2, 0), jnp.moveaxis(k_in, 2, 0), jnp.moveaxis(v, 2, 0), decay)

    def step(s, inp):
        q_n, k_n, v_n, a_n = inp
        o_n = jnp.einsum('bhcd,bhde->bhce', q_n, s)
        s = a_n * s + jnp.einsum('bhcd,bhce->bhde', k_n, v_n)
        return s, o_n

    s, o = lax.scan(step, s0, xs)
    return jnp.moveaxis(o, 0, 2), s


def gla_chunked(q, k, v, log_f, s0):
    b, h, l, dk = q.shape
    dv = v.shape[-1]
    n = l // CHUNK
    qc, kc, lc = (a.reshape(b, h, n, CHUNK, dk) for a in (q, k, log_f))
    vc = v.reshape(b, h, n, CHUNK, dv)
    cum = jnp.cumsum(lc, axis=3)
    ref = cum[:, :, :, CHUNK // 2:CHUNK // 2 + 1]
    scores = jnp.einsum('bhntd,bhnsd->bhnts', qc * jnp.exp(cum - ref), kc * jnp.exp(ref - cum))
    lower = jnp.tril(jnp.ones((CHUNK, CHUNK), dtype=bool))
    o_intra = jnp.einsum('bhnts,bhnse->bhnte', jnp.where(lower, scores, 0.0), vc)
    cum_last = cum[:, :, :, -1:]
    decay = jnp.moveaxis(jnp.exp(cum_last[:, :, :, 0]), 2, 0)[..., None]
    o_inter, s = chunk_scan(qc * jnp.exp(cum), kc * jnp.exp(cum_last - cum), vc, decay, s0)
    return (o_intra + o_inter).reshape(b, h, l, dv), s


def gla_final_state(k, v, log_f):
    cum = jnp.cumsum(log_f, axis=2)
    return jnp.einsum('bhld,bhle->bhde', k * jnp.exp(cum[:, :, -1:] - cum), v)


def retention_chunked(q, k, v, log_gamma, s0):
    b, h, l, dk = q.shape
    dv = v.shape[-1]
    n = l // CHUNK
    qc = q.reshape(b, h, n, CHUNK, dk)
    kc = k.reshape(b, h, n, CHUNK, dk)
    vc = v.reshape(b, h, n, CHUNK, dv)
    pos = jnp.arange(CHUNK, dtype=F32)
    rel = pos[:, None] - pos[None, :]
    dmat = jnp.where(rel >= 0, jnp.exp(log_gamma[:, None, None] * jnp.maximum(rel, 0.0)), 0.0)
    scores = jnp.einsum('bhntd,bhnsd->bhnts', qc, kc) * dmat[None, :, None]
    o_intra = jnp.einsum('bhnts,bhnse->bhnte', scores, vc)
    lg = log_gamma[:, None]
    q_in = qc * jnp.exp(lg * (pos + 1.0))[None, :, None, :, None]
    k_in = kc * jnp.exp(lg * (CHUNK - 1.0 - pos))[None, :, None, :, None]
    decay = jnp.broadcast_to(jnp.exp(log_gamma * CHUNK)[None, None, :, None, None], (n, 1, h, 1, 1))
    o_inter, s = chunk_scan(q_in, k_in, vc, decay, s0)
    return (o_intra + o_inter).reshape(b, h, l, dv), s


def retention_final_state(k, v, log_gamma):
    lc = k.shape[2]
    w = jnp.exp(log_gamma[:, None] * (lc - 1.0 - jnp.arange(lc, dtype=F32)))
    return jnp.einsum('bhld,bhle->bhde', k * w[None, :, :, None], v)


def gated_head_norm(o, g_raw, gain=None):
    b, h, l, dv = o.shape
    y = rms_norm(from_heads(o), gain) * jax.nn.silu(g_raw.reshape(b, l, h, dv).astype(F32))
    return y.reshape(b, l, h * dv)


def even_mixer(h_ctx, h_lat, w_in, w_out, qk_g, sink, out_g, lb, cos, sin, need_ctx):
    dt = h_lat.dtype
    n_b = h_ctx.shape[0]
    sink = sink.astype(F32).reshape(ATTN_KV_HEADS, ATTN_GROUP)
    lb = lb.reshape(HGRN_HEADS, 1, HGRN_DK)
    p = split_parts(h_lat @ w_in, EVEN_PARTS)
    names = [n for n, _ in EVEN_PARTS] if need_ctx else ['a_k', 'a_v', 'b_ff', 'b_fb', 'b_i']
    pc = project_parts(h_ctx, w_in, EVEN_PARTS, names)

    def gates(f_raw):
        f = lb + (1.0 - lb) * jax.nn.sigmoid(to_heads(f_raw, HGRN_HEADS).astype(F32))
        return 1.0 - f, jnp.log(f)

    k_ctx = kv_heads(pc['a_k'], qk_g[1])
    v_ctx = kv_heads(pc['a_v'])
    q_lat = apply_rope(attn_q_heads(p['a_q'], qk_g[0]), cos, sin)
    k_lat = apply_rope(kv_heads(p['a_k'], qk_g[1]), cos, sin)
    a_lat = window_attention(q_lat, k_lat, kv_heads(p['a_v']), k_ctx, v_ctx, sink)

    k_fw_c, lf_fw_c = gates(pc['b_ff'])
    k_bw_c, lf_bw_c = gates(pc['b_fb'])
    i_c = to_heads(pc['b_i'], HGRN_HEADS).astype(F32)
    if need_ctx:
        zeros = jnp.zeros((n_b, HGRN_HEADS, HGRN_DK, HGRN_DV), F32)
        q_c = jax.nn.silu(to_heads(pc['b_q'], HGRN_HEADS).astype(F32))
        o_fw_c, s_fw = gla_chunked(q_c, k_fw_c, i_c, lf_fw_c, zeros)
        o_bw_c, s_bw = gla_chunked(flip(q_c), flip(k_bw_c), flip(i_c), flip(lf_bw_c), zeros)
    else:
        s_fw = gla_final_state(k_fw_c, i_c, lf_fw_c)
        s_bw = gla_final_state(flip(k_bw_c), flip(i_c), flip(lf_bw_c))
    k_fw, lf_fw = gates(p['b_ff'])
    k_bw, lf_bw = gates(p['b_fb'])
    q_l = jax.nn.silu(to_heads(p['b_q'], HGRN_HEADS).astype(F32))
    i_l = to_heads(p['b_i'], HGRN_HEADS).astype(F32)
    o_fw, _ = gla_chunked(q_l, k_fw, i_l, lf_fw, s_fw)
    o_bw, _ = gla_chunked(flip(q_l), flip(k_bw), flip(i_l), flip(lf_bw), s_bw)
    b_lat = gated_head_norm(o_fw + flip(o_bw), p['b_g'], out_g)

    y_lat = jnp.concatenate([a_lat.astype(dt), b_lat.astype(dt)], axis=-1) @ w_out
    if not need_ctx:
        return None, y_lat
    a_ctx = context_attention(attn_q_heads(pc['a_q'], qk_g[0]), k_ctx, v_ctx, sink)
    b_ctx = gated_head_norm(o_fw_c + flip(o_bw_c), pc['b_g'], out_g)
    y_ctx = jnp.concatenate([a_ctx.astype(dt), b_ctx.astype(dt)], axis=-1) @ w_out
    return y_ctx, y_lat


def odd_mixer(h_ctx, h_lat, w_in, w_out, cos, sin, need_ctx):
    dt = h_lat.dtype
    n_b = h_ctx.shape[0]
    log_g_fw = jnp.log(1.0 - 2.0 ** (-5.0 - jnp.arange(RET_HEADS, dtype=F32)))
    log_g_bw = log_g_fw[::-1]
    k_scale = RET_DK ** -0.5
    p = split_parts(h_lat @ w_in, ODD_PARTS)
    names = [n for n, _ in ODD_PARTS] if need_ctx else ['k', 'v']
    pc = project_parts(h_ctx, w_in, ODD_PARTS, names)

    k_c = to_heads(pc['k'], RET_HEADS).astype(F32) * k_scale
    v_c = to_heads(pc['v'], RET_HEADS).astype(F32)
    if need_ctx:
        zeros = jnp.zeros((n_b, RET_HEADS, RET_DK, RET_DV), F32)
        q_c = to_heads(pc['q'], RET_HEADS).astype(F32)
        o_fw_c, s_fw = retention_chunked(q_c, k_c, v_c, log_g_fw, zeros)
        o_bw_c, s_bw = retention_chunked(flip(q_c), flip(k_c), flip(v_c), log_g_bw, zeros)
    else:
        s_fw = retention_final_state(k_c, v_c, log_g_fw)
        s_bw = retention_final_state(flip(k_c), flip(v_c), log_g_bw)

    q_l = apply_rope(to_heads(p['q'], RET_HEADS).astype(F32), cos, sin)
    k_l = apply_rope(to_heads(p['k'], RET_HEADS).astype(F32), cos, sin) * k_scale
    v_l = to_heads(p['v'], RET_HEADS).astype(F32)
    o_fw, _ = retention_chunked(q_l, k_l, v_l, log_g_fw, s_fw)
    o_bw, _ = retention_chunked(flip(q_l), flip(k_l), flip(v_l), log_g_bw, s_bw)
    y_lat = gated_head_norm(o_fw + flip(o_bw), p['g']).astype(dt) @ w_out
    if not need_ctx:
        return None, y_lat
    y_ctx = gated_head_norm(o_fw_c + flip(o_bw_c), pc['g']).astype(dt) @ w_out
    return y_ctx, y_lat


def swiglu(h, w_in, w_out):
    gate, up = jnp.split(h @ w_in, 2, axis=-1)
    return (jax.nn.silu(gate) * up) @ w_out


def setup_inputs(seed: int = 0) -> dict:
    key = jax.random.key(seed)
    ks = jax.random.split(key, 17)
    d = D_MODEL

    def nrm(k, shape, scale):
        return jax.random.normal(k, shape, F32) * scale

    return {
        'x': nrm(ks[0], (BATCH, SEQ, d), 1.0),
        'c': nrm(ks[1], (BATCH, d), 1.0),
        'ctx': nrm(ks[2], (BATCH, CTX_LEN, d), 1.0),
        'c_ctx': nrm(ks[3], (d,), 1.0),
        'mod_w': nrm(ks[4], (DEPTH, d, 6 * d), 0.5 * d ** -0.5),
        'mod_b': nrm(ks[5], (DEPTH, 6 * d), 0.02),
        'norm_g': 1.0 + nrm(ks[6], (DEPTH, 2, d), 0.02),
        'ffn_w_in': nrm(ks[7], (DEPTH, d, 2 * FFN_HIDDEN), d ** -0.5),
        'ffn_w_out': nrm(ks[8], (DEPTH, FFN_HIDDEN, d), FFN_HIDDEN ** -0.5),
        'even_w_in': nrm(ks[9], (N_EVEN, d, EVEN_IN), d ** -0.5),
        'even_w_out': nrm(ks[10], (N_EVEN, EVEN_OUT, d), EVEN_OUT ** -0.5),
        'attn_qk_norm_g': 1.0 + nrm(ks[11], (N_EVEN, 2, HEAD_DIM), 0.02),
        'attn_sink': nrm(ks[12], (N_EVEN, ATTN_HEADS), 0.5),
        'hgrn_out_norm_g': 1.0 + nrm(ks[13], (N_EVEN, HGRN_DV), 0.02),
        'hgrn_lb': nrm(ks[14], (N_EVEN + 1, HGRN_WIDTH), 0.1),
        'odd_w_in': nrm(ks[15], (N_ODD, d, ODD_IN), d ** -0.5),
        'odd_w_out': nrm(ks[16], (N_ODD, ODD_OUT, d), ODD_OUT ** -0.5),
    }


def reference(x, c, ctx, c_ctx, mod_w, mod_b, norm_g, ffn_w_in, ffn_w_out, even_w_in, even_w_out,
              attn_qk_norm_g, attn_sink, hgrn_out_norm_g, hgrn_lb, odd_w_in, odd_w_out):
    n_tok = x.shape[1]
    rope_cos, rope_sin = axial_rope_tables(n_tok)
    ret_cos, ret_sin = retention_rope_tables(n_tok)
    lower_bounds = jnp.cumsum(jax.nn.softmax(hgrn_lb.astype(F32), axis=0), axis=0)
    cond_lat = jax.nn.silu(c)
    cond_ctx = jax.nn.silu(c_ctx)
    for layer in range(DEPTH):
        last = layer == DEPTH - 1
        j = layer // 2
        m_lat = jnp.split((cond_lat @ mod_w[layer] + mod_b[layer])[:, None, :], 6, axis=-1)
        m_ctx = jnp.split((cond_ctx @ mod_w[layer] + mod_b[layer])[None, None, :], 6, axis=-1)
        h_lat = modulate(rms_norm(x, norm_g[layer, 0]), m_lat[0], m_lat[1])
        h_ctx = modulate(rms_norm(ctx, norm_g[layer, 0]), m_ctx[0], m_ctx[1])
        if layer % 2 == 0:
            y_ctx, y_lat = even_mixer(h_ctx, h_lat, even_w_in[j], even_w_out[j], attn_qk_norm_g[j],
                                      attn_sink[j], hgrn_out_norm_g[j], lower_bounds[j],
                                      rope_cos, rope_sin, not last)
        else:
            y_ctx, y_lat = odd_mixer(h_ctx, h_lat, odd_w_in[j], odd_w_out[j], ret_cos, ret_sin, not last)
        x = x + m_lat[2] * y_lat
        x = x + m_lat[5] * swiglu(modulate(rms_norm(x, norm_g[layer, 1]), m_lat[3], m_lat[4]),
                                  ffn_w_in[layer], ffn_w_out[layer])
        if not last:
            ctx = ctx + m_ctx[2] * y_ctx
            ctx = ctx + m_ctx[5] * swiglu(modulate(rms_norm(ctx, norm_g[layer, 1]), m_ctx[3], m_ctx[4]),
                                          ffn_w_in[layer], ffn_w_out[layer])
    return x
```

```python
import functools
import math

import numpy as np
import jax
import jax.numpy as jnp
from jax import lax
from jax.experimental import pallas as pl
from jax.experimental.pallas import tpu as pltpu

F32 = jnp.float32
MXU_DTYPE = jnp.bfloat16

D_MODEL = 1024
GRID_W = 64
HEAD_DIM = 64
ATTN_HEADS = 8
ATTN_KV_HEADS = 2
WINDOW = 128
ROPE_BASE = 10000.0
HGRN_HEADS = 4
HGRN_DK = 128
HGRN_WIDTH = HGRN_HEADS * HGRN_DK
GLA_CHUNK = 64
RET_HEADS = 4
RET_DK = 256
RET_DV = 512
RET_BASE = 10000.0
FFN_HIDDEN = 2816
A_Q = ATTN_HEADS * HEAD_DIM
A_KV = ATTN_KV_HEADS * HEAD_DIM
EVEN_IN = A_Q + 2 * A_KV + 5 * HGRN_WIDTH
HG_OFF = A_Q + 2 * A_KV
ODD_Q = RET_HEADS * RET_DK
ODD_V = RET_HEADS * RET_DV
ODD_IN = 2 * ODD_Q + 2 * ODD_V
EPS = 1e-6
NEG = -1e30

LANES = 128
FFN_CHUNK = 256
VMEM_LIMIT = 56 * 1024 * 1024


def _mx(a):
    return a.astype(MXU_DTYPE)


def _dot(a, b):
    return jnp.dot(_mx(a), _mx(b), preferred_element_type=F32)


def _dot_nt(a, b):
    return lax.dot_general(_mx(a), _mx(b), (((1,), (1,)), ((), ())), preferred_element_type=F32)


def _dot_tn(a, b):
    return lax.dot_general(_mx(a), _mx(b), (((0,), (0,)), ((), ())), preferred_element_type=F32)


def _split(x):
    hi = _mx(x)
    lo = _mx(x - hi.astype(F32))
    return hi, lo


def _sigmoid(x):
    return 1.0 / (1.0 + jnp.exp(-x))


def _silu(x):
    return x * _sigmoid(x)


def _norm_mod(x, g, shift, scale):
    y = x * lax.rsqrt(jnp.mean(x * x, axis=-1, keepdims=True) + EPS) * g
    return y * (1.0 + scale) + shift


def _cparams(sem):
    return pltpu.CompilerParams(dimension_semantics=sem, vmem_limit_bytes=VMEM_LIMIT)


def _const_spec(shape):
    nd = len(shape)
    return pl.BlockSpec(shape, lambda *_: (0,) * nd, pipeline_mode=pl.Buffered(1))


def _mod_kernel(c_ref, w_ref, b_ref, o_ref):
    s = _silu(c_ref[...])
    o_ref[0] = jnp.dot(s, w_ref[0], preferred_element_type=F32,
                       precision=lax.Precision.HIGHEST) + b_ref[0]


def _modulation(cond, mod_w, mod_b):
    depth, d, n = mod_w.shape
    rows = cond.shape[0]
    tn = 1536
    return pl.pallas_call(
        _mod_kernel,
        out_shape=jax.ShapeDtypeStruct((depth, rows, n), F32),
        grid=(depth, n // tn),
        in_specs=[pl.BlockSpec((rows, d), lambda l, j: (0, 0)),
                  pl.BlockSpec((1, d, tn), lambda l, j: (l, 0, j)),
                  pl.BlockSpec((1, 1, tn), lambda l, j: (l, 0, j))],
        out_specs=pl.BlockSpec((1, rows, tn), lambda l, j: (l, 0, j)),
        compiler_params=_cparams(("arbitrary", "arbitrary")),
    )(cond, mod_w, mod_b.reshape(depth, 1, n))


def _head_rms(u, bd, gain):
    hi, lo = _split(u * u)
    ms = jnp.dot(hi, bd, preferred_element_type=F32) + jnp.dot(lo, bd, preferred_element_type=F32)
    return u * lax.rsqrt(ms + EPS) * gain


def _rope_half_swap(u):
    lane = lax.broadcasted_iota(jnp.int32, u.shape, 1)
    first = (lane % HEAD_DIM) < (HEAD_DIM // 2)
    return jnp.where(first, pltpu.roll(u, LANES - HEAD_DIM // 2, 1), pltpu.roll(u, HEAD_DIM // 2, 1))


def _dup_halves(u):
    lane = lax.broadcasted_iota(jnp.int32, u.shape, 1)
    low = lane < HEAD_DIM
    swapped = pltpu.roll(u, HEAD_DIM, 1)
    return jnp.concatenate([jnp.where(low, u, swapped), jnp.where(low, swapped, u)], axis=1)


def _proj_even_kernel(x_ref, mod_ref, ng_ref, w_ref, gq_ref, gk_ref, bdq_ref, bdk_ref, cs_ref, sn_ref,
                      q_ref, k_ref, v_ref, hg_ref, *, rope):
    h = _mx(_norm_mod(x_ref[0], ng_ref[...], mod_ref[0:1, :], mod_ref[1:2, :]))
    qkv = jnp.dot(h, w_ref[:, :HG_OFF], preferred_element_type=F32)
    q = _head_rms(qkv[:, :A_Q], bdq_ref[...], gq_ref[...])
    k = _head_rms(qkv[:, A_Q:A_Q + A_KV], bdk_ref[...], gk_ref[...])
    v = qkv[:, A_Q + A_KV:HG_OFF]
    if rope:
        cs, sn = cs_ref[...], sn_ref[...]
        q = jnp.concatenate(
            [q[:, c * LANES:(c + 1) * LANES] * cs + _rope_half_swap(q[:, c * LANES:(c + 1) * LANES]) * sn
             for c in range(A_Q // LANES)], axis=1)
        k = k * cs + _rope_half_swap(k) * sn
    q_ref[0] = (q * HEAD_DIM ** -0.5).astype(q_ref.dtype)
    k_ref[0] = _dup_halves(k).astype(k_ref.dtype)
    v_ref[0] = _dup_halves(v).astype(v_ref.dtype)
    hg_ref[0] = jnp.dot(h, w_ref[:, HG_OFF:], preferred_element_type=F32)


def _proj_even(x, mod4, layer, row_of_batch, ng, w, gq, gk, bdq, bdk, cs, sn, *, rope, tm):
    b, l, d = x.shape
    nblk = l // tm
    kern = functools.partial(_proj_even_kernel, rope=rope)
    tab_spec = pl.BlockSpec((tm, LANES), lambda i, j: (j, 0))
    return pl.pallas_call(
        kern,
        out_shape=(jax.ShapeDtypeStruct((b, l, A_Q), MXU_DTYPE),
                   jax.ShapeDtypeStruct((b, l, 2 * A_KV), MXU_DTYPE),
                   jax.ShapeDtypeStruct((b, l, 2 * A_KV), MXU_DTYPE),
                   jax.ShapeDtypeStruct((b, l, 5 * HGRN_WIDTH), F32)),
        grid=(b, nblk),
        in_specs=[pl.BlockSpec((1, tm, d), lambda i, j: (i, j, 0)),
                  pl.BlockSpec((None, None, 6, d), lambda i, j: (layer, row_of_batch(i), 0, 0)),
                  _const_spec((1, d)),
                  _const_spec((d, EVEN_IN)),
                  _const_spec((1, A_Q)), _const_spec((1, A_KV)),
                  _const_spec((A_Q, A_Q)), _const_spec((A_KV, A_KV)),
                  tab_spec, tab_spec],
        out_specs=(pl.BlockSpec((1, tm, A_Q), lambda i, j: (i, j, 0)),
                   pl.BlockSpec((1, tm, 2 * A_KV), lambda i, j: (i, j, 0)),
                   pl.BlockSpec((1, tm, 2 * A_KV), lambda i, j: (i, j, 0)),
                   pl.BlockSpec((1, tm, 5 * HGRN_WIDTH), lambda i, j: (i, j, 0))),
        compiler_params=_cparams(("parallel", "parallel")),
    )(x, mod4, ng, w, gq, gk, bdq, bdk, cs, sn)


def _attn_kernel(*refs, window, tq, seq):
    if window:
        (q_ref, kp_ref, kc_ref, kn_ref, vp_ref, vc_ref, vn_ref, kx_ref, vx_ref, sink_ref, o_ref) = refs
    else:
        (q_ref, kx_ref, vx_ref, sink_ref, o_ref) = refs
    i = pl.program_id(1)
    n_ctx = kx_ref.shape[1]
    if window:
        span = 3 * tq
        qpos = i * tq + lax.broadcasted_iota(jnp.int32, (tq, span), 0)
        kpos = (i - 1) * tq + lax.broadcasted_iota(jnp.int32, (tq, span), 1)
        ok = (jnp.abs(kpos - qpos) <= WINDOW) & (kpos >= 0) & (kpos < seq)
        bias = jnp.concatenate([jnp.where(ok, 0.0, NEG), jnp.zeros((tq, n_ctx), F32)], axis=1)
        bias = jnp.concatenate([bias] * 4, axis=0)
    lane = lax.broadcasted_iota(jnp.int32, (2 * tq, LANES), 1)
    low = lane < HEAD_DIM
    q = q_ref[0]
    for g in range(ATTN_KV_HEADS):
        cols = slice(g * LANES, (g + 1) * LANES)
        if window:
            kk = jnp.concatenate([kp_ref[0, :, cols], kc_ref[0, :, cols], kn_ref[0, :, cols],
                                  kx_ref[0, :, cols]], axis=0)
            vv = jnp.concatenate([vp_ref[0, :, cols], vc_ref[0, :, cols], vn_ref[0, :, cols],
                                  vx_ref[0, :, cols]], axis=0)
        else:
            kk, vv = kx_ref[0, :, cols], vx_ref[0, :, cols]
        q2 = jnp.concatenate([q[:, (2 * g) * LANES:(2 * g + 1) * LANES],
                              q[:, (2 * g + 1) * LANES:(2 * g + 2) * LANES]], axis=0)
        zero = jnp.zeros_like(q2)
        q4 = jnp.concatenate([jnp.where(low, q2, zero), jnp.where(low, zero, q2)], axis=0)
        s = _dot_nt(q4, kk)
        if window:
            s = s + bias
        sink = sink_ref[g]
        m = jnp.maximum(jnp.max(s, axis=-1, keepdims=True), sink)
        p = jnp.exp(s - m)
        denom = jnp.sum(p, axis=-1, keepdims=True) + jnp.exp(sink - m)
        o = _dot(p, vv) * (1.0 / denom)
        o2 = jnp.where(low, o[:2 * tq], o[2 * tq:])
        o_ref[0, :, (2 * g) * LANES:(2 * g + 1) * LANES] = o2[:tq].astype(o_ref.dtype)
        o_ref[0, :, (2 * g + 1) * LANES:(2 * g + 2) * LANES] = o2[tq:].astype(o_ref.dtype)


def _sink_rows(sink, tq):
    s = sink.astype(F32).reshape(ATTN_KV_HEADS, 4)
    order = jnp.array([0, 2, 1, 3])
    return jnp.repeat(s[:, order], tq, axis=1)[:, :, None]


def _attention(q, k, v, kx, vx, sink, *, window, tq):
    b, l, _ = q.shape
    n_ctx = kx.shape[1]
    nq = l // tq
    kern = functools.partial(_attn_kernel, window=window, tq=tq, seq=l)
    qspec = pl.BlockSpec((1, tq, A_Q), lambda i, j: (i, j, 0))
    xspec = pl.BlockSpec((1, n_ctx, 2 * A_KV), lambda i, j: (i, 0, 0))
    sspec = _const_spec((ATTN_KV_HEADS, 4 * tq, 1))
    if window:
        prev = pl.BlockSpec((1, tq, 2 * A_KV), lambda i, j: (i, jnp.maximum(j - 1, 0), 0))
        cur = pl.BlockSpec((1, tq, 2 * A_KV), lambda i, j: (i, j, 0))
        nxt = pl.BlockSpec((1, tq, 2 * A_KV), lambda i, j: (i, jnp.minimum(j + 1, nq - 1), 0))
        in_specs = [qspec, prev, cur, nxt, prev, cur, nxt, xspec, xspec, sspec]
        args = (q, k, k, k, v, v, v, kx, vx, _sink_rows(sink, tq))
    else:
        in_specs = [qspec, xspec, xspec, sspec]
        args = (q, kx, vx, _sink_rows(sink, tq))
    return pl.pallas_call(
        kern,
        out_shape=jax.ShapeDtypeStruct((b, l, A_Q), MXU_DTYPE),
        grid=(b, nq),
        in_specs=in_specs,
        out_specs=pl.BlockSpec((1, tq, A_Q), lambda i, j: (i, j, 0)),
        compiler_params=_cparams(("parallel", "parallel")),
    )(*args)


def _hgrn_kernel(q_ref, f_ref, i_ref, lb_ref, s0_ref, o_ref, sfin_ref, st_ref, *, rev, tb):
    j = pl.program_id(1)
    nc = tb // GLA_CHUNK
    c_len = GLA_CHUNK
    width = HGRN_WIDTH

    @pl.when(j == 0)
    def _():
        st_ref[...] = s0_ref[0]

    lb = lb_ref[...]
    qs = _silu(q_ref[0])
    f = lb + (1.0 - lb) * _sigmoid(f_ref[0])
    kf = 1.0 - f
    lf = jnp.log(f)
    iv = _mx(i_ref[0])

    r = lax.broadcasted_iota(jnp.int32, (tb, tb), 0)
    c = lax.broadcasted_iota(jnp.int32, (tb, tb), 1)
    same = (r // c_len) == (c // c_len)
    lower = same & (c <= r)
    tri = _mx(jnp.where(lower, 1.0, 0.0))
    hi, lo = _split(lf)
    cum = (jnp.dot(tri, hi, preferred_element_type=F32)
           + jnp.dot(tri, lo, preferred_element_type=F32)).reshape(nc, c_len, width)
    if rev:
        w3 = cum[:, c_len - 1:c_len, :] - cum + lf.reshape(nc, c_len, width)
        edge = w3[:, 0:1, :]
        keep = same & (c >= r)
    else:
        w3 = cum
        edge = cum[:, c_len - 1:c_len, :]
        keep = lower
    mid = w3[:, c_len // 2:c_len // 2 + 1, :]
    qs3 = qs.reshape(nc, c_len, width)
    kf3 = kf.reshape(nc, c_len, width)
    qa = _mx((qs3 * jnp.exp(w3 - mid)).reshape(tb, width))
    kb = _mx((kf3 * jnp.exp(mid - w3)).reshape(tb, width))
    qe = _mx((qs3 * jnp.exp(w3)).reshape(tb, width))
    ke = _mx((kf3 * jnp.exp(edge - w3)).reshape(tb, width))
    dec = jnp.exp(edge)

    order = range(nc - 1, -1, -1) if rev else range(nc)
    for h in range(HGRN_HEADS):
        hs = slice(h * HGRN_DK, (h + 1) * HGRN_DK)
        s = jnp.where(keep, _dot_nt(qa[:, hs], kb[:, hs]), 0.0)
        o_intra = _dot(s, iv[:, hs])
        st = st_ref[h]
        for ci in order:
            rows = slice(ci * c_len, (ci + 1) * c_len)
            o_ref[0, rows, hs] = o_intra[rows] + _dot_nt(qe[rows, hs], st)
            st = st * dec[ci, :, hs] + _dot_tn(iv[rows, hs], ke[rows, hs])
        st_ref[h] = st

    @pl.when(j == pl.num_programs(1) - 1)
    def _():
        sfin_ref[0] = st_ref[...]


def _hgrn_scan(hg, lb, s0, *, rev, tb):
    b, l, _ = hg.shape
    nb = l // tb
    blk = (lambda j: nb - 1 - j) if rev else (lambda j: j)
    fcol = 2 if rev else 1
    kern = functools.partial(_hgrn_kernel, rev=rev, tb=tb)
    st_shape = (HGRN_HEADS, HGRN_DK, HGRN_DK)
    return pl.pallas_call(
        kern,
        out_shape=(jax.ShapeDtypeStruct((b, l, HGRN_WIDTH), F32),
                   jax.ShapeDtypeStruct((b,) + st_shape, F32)),
        grid=(b, nb),
        in_specs=[pl.BlockSpec((1, tb, HGRN_WIDTH), lambda i, j: (i, blk(j), 0)),
                  pl.BlockSpec((1, tb, HGRN_WIDTH), lambda i, j: (i, blk(j), fcol)),
                  pl.BlockSpec((1, tb, HGRN_WIDTH), lambda i, j: (i, blk(j), 3)),
                  _const_spec((1, HGRN_WIDTH)),
                  pl.BlockSpec((1,) + st_shape, lambda i, j: (i, 0, 0, 0))],
        out_specs=(pl.BlockSpec((1, tb, HGRN_WIDTH), lambda i, j: (i, blk(j), 0)),
                   pl.BlockSpec((1,) + st_shape, lambda i, j: (i, 0, 0, 0))),
        scratch_shapes=[pltpu.VMEM(st_shape, F32)],
        compiler_params=_cparams(("parallel", "arbitrary")),
    )(hg, hg, hg, lb, s0)


def _swiglu_residual(x1, mod_ref, ng2_ref, wg_ref, wu_ref, wd_ref, acc_ref):
    h2 = _mx(_norm_mod(x1, ng2_ref[...], mod_ref[3:4, :], mod_ref[4:5, :]))
    acc_ref[...] = jnp.zeros_like(acc_ref)

    def body(ci, carry):
        gate = jnp.dot(h2, wg_ref[ci], preferred_element_type=F32)
        up = jnp.dot(h2, wu_ref[ci], preferred_element_type=F32)
        acc_ref[...] += _dot(_silu(gate) * up, wd_ref[ci])
        return carry

    lax.fori_loop(0, wg_ref.shape[0], body, 0)
    return x1 + mod_ref[5:6, :] * acc_ref[...]


def _out_even_kernel(a_ref, of_ref, ob_ref, g_ref, x_ref, mod_ref, og_ref, ng2_ref, wo_ref,
                     wg_ref, wu_ref, wd_ref, o_ref, acc_ref):
    o = of_ref[0] + ob_ref[0]
    og = og_ref[...]
    parts = []
    for h in range(HGRN_HEADS):
        oh = o[:, h * HGRN_DK:(h + 1) * HGRN_DK]
        parts.append(oh * lax.rsqrt(jnp.mean(oh * oh, axis=-1, keepdims=True) + EPS) * og)
    bmix = jnp.concatenate(parts, axis=1) * _silu(g_ref[0])
    y = (jnp.dot(a_ref[0], wo_ref[:A_Q, :], preferred_element_type=F32)
         + _dot(bmix, wo_ref[A_Q:, :]))
    x1 = x_ref[0] + mod_ref[2:3, :] * y
    o_ref[0] = _swiglu_residual(x1, mod_ref, ng2_ref, wg_ref, wu_ref, wd_ref, acc_ref)


def _out_even(a, o_fw, o_bw, hg, x, mod4, layer, row_of_batch, og, ng2, wo, wg, wu, wd, *, tm):
    b, l, d = x.shape
    nch = wg.shape[0]
    row = lambda w: pl.BlockSpec((1, tm, w), lambda i, j: (i, j, 0))
    return pl.pallas_call(
        _out_even_kernel,
        out_shape=jax.ShapeDtypeStruct((b, l, d), F32),
        grid=(b, l // tm),
        in_specs=[row(A_Q), row(HGRN_WIDTH), row(HGRN_WIDTH),
                  pl.BlockSpec((1, tm, HGRN_WIDTH), lambda i, j: (i, j, 4)),
                  row(d),
                  pl.BlockSpec((None, None, 6, d), lambda i, j: (layer, row_of_batch(i), 0, 0)),
                  _const_spec((1, HGRN_DK)), _const_spec((1, d)),
                  _const_spec((A_Q + HGRN_WIDTH, d)),
                  _const_spec((nch, d, FFN_CHUNK)), _const_spec((nch, d, FFN_CHUNK)),
                  _const_spec((nch, FFN_CHUNK, d))],
        out_specs=row(d),
        scratch_shapes=[pltpu.VMEM((tm, d), F32)],
        compiler_params=_cparams(("parallel", "parallel")),
    )(a, o_fw, o_bw, hg, x, mod4, og, ng2, wo, wg, wu, wd)


def _rope_ret(u, cs, sn):
    parts = []
    for h in range(RET_HEADS):
        x1 = u[:, h * RET_DK:h * RET_DK + LANES]
        x2 = u[:, h * RET_DK + LANES:(h + 1) * RET_DK]
        parts += [x1 * cs - x2 * sn, x2 * cs + x1 * sn]
    return jnp.concatenate(parts, axis=1)


def _proj_odd_kernel(x_ref, mod_ref, ng_ref, w_ref, cs_ref, sn_ref, q_ref, k_ref, v_ref, g_ref):
    h = _mx(_norm_mod(x_ref[0], ng_ref[...], mod_ref[0:1, :], mod_ref[1:2, :]))
    cs, sn = cs_ref[...], sn_ref[...]
    q = jnp.dot(h, w_ref[:, :ODD_Q], preferred_element_type=F32)
    q_ref[0] = _rope_ret(q, cs, sn).astype(q_ref.dtype)
    k = jnp.dot(h, w_ref[:, ODD_Q:2 * ODD_Q], preferred_element_type=F32)
    k_ref[0] = (_rope_ret(k, cs, sn) * RET_DK ** -0.5).astype(k_ref.dtype)
    v_ref[0] = jnp.dot(h, w_ref[:, 2 * ODD_Q:2 * ODD_Q + ODD_V], preferred_element_type=F32).astype(v_ref.dtype)
    g_ref[0] = jnp.dot(h, w_ref[:, 2 * ODD_Q + ODD_V:], preferred_element_type=F32)


def _proj_odd(x, mod4, layer, ng, w, cs, sn, *, tm):
    b, l, d = x.shape
    row = lambda wd: pl.BlockSpec((1, tm, wd), lambda i, j: (i, j, 0))
    tab = pl.BlockSpec((tm, LANES), lambda i, j: (j, 0))
    return pl.pallas_call(
        _proj_odd_kernel,
        out_shape=(jax.ShapeDtypeStruct((b, l, ODD_Q), MXU_DTYPE),
                   jax.ShapeDtypeStruct((b, l, ODD_Q), MXU_DTYPE),
                   jax.ShapeDtypeStruct((b, l, ODD_V), MXU_DTYPE),
                   jax.ShapeDtypeStruct((b, l, ODD_V), F32)),
        grid=(b, l // tm),
        in_specs=[row(d),
                  pl.BlockSpec((None, None, 6, d), lambda i, j: (layer, i, 0, 0)),
                  _const_spec((1, d)), _const_spec((d, ODD_IN)), tab, tab],
        out_specs=(row(ODD_Q), row(ODD_Q), row(ODD_V), row(ODD_V)),
        compiler_params=_cparams(("parallel", "parallel")),
    )(x, mod4, ng, w, cs, sn)


def _proj_ctx_kv_kernel(x_ref, mod_ref, ng_ref, w_ref, k_ref, v_ref):
    h = _mx(_norm_mod(x_ref[0], ng_ref[...], mod_ref[0:1, :], mod_ref[1:2, :]))
    k = jnp.dot(h, w_ref[:, :ODD_Q], preferred_element_type=F32)
    k_ref[0] = (k * RET_DK ** -0.5).astype(k_ref.dtype)
    v_ref[0] = jnp.dot(h, w_ref[:, ODD_Q:], preferred_element_type=F32).astype(v_ref.dtype)


def _proj_ctx_kv(ctx, mod4, layer, ctx_row, ng, w_kv):
    b, lc, d = ctx.shape
    return pl.pallas_call(
        _proj_ctx_kv_kernel,
        out_shape=(jax.ShapeDtypeStruct((b, lc, ODD_Q), MXU_DTYPE),
                   jax.ShapeDtypeStruct((b, lc, ODD_V), MXU_DTYPE)),
        grid=(b,),
        in_specs=[pl.BlockSpec((1, lc, d), lambda i: (i, 0, 0)),
                  pl.BlockSpec((None, None, 6, d), lambda i: (layer, ctx_row, 0, 0)),
                  _const_spec((1, d)), _const_spec((d, ODD_Q + ODD_V))],
        out_specs=(pl.BlockSpec((1, lc, ODD_Q), lambda i: (i, 0, 0)),
                   pl.BlockSpec((1, lc, ODD_V), lambda i: (i, 0, 0))),
        compiler_params=_cparams(("parallel",)),
    )(ctx, mod4, ng, w_kv)


def _ret_log_gamma(h, rev):
    hh = RET_HEADS - 1 - h if rev else h
    return math.log(1.0 - 2.0 ** (-5.0 - hh))


def _ret_kernel(*refs, rev, tb):
    if rev:
        q_ref, k_ref, v_ref, kc_ref, vc_ref, o_ref, s_ref = refs
    else:
        q_ref, k_ref, v_ref, kc_ref, vc_ref, dm_ref, o_ref, s_ref = refs
    j = pl.program_id(1)
    lc = kc_ref.shape[1]

    def weights(n, lg):
        pos = lax.broadcasted_iota(jnp.int32, (n, 1), 0).astype(F32)
        if rev:
            return jnp.exp(lg * pos), jnp.exp(lg * (n - pos))
        return jnp.exp(lg * (n - 1.0 - pos)), jnp.exp(lg * (pos + 1.0))

    @pl.when(j == 0)
    def _():
        for h in range(RET_HEADS):
            kw, _ = weights(lc, _ret_log_gamma(h, rev))
            kc = kc_ref[0, :, h * RET_DK:(h + 1) * RET_DK].astype(F32) * kw
            s_ref[h] = _dot_tn(kc, vc_ref[0, :, h * RET_DV:(h + 1) * RET_DV])

    for h in range(RET_HEADS):
        lg = _ret_log_gamma(h, rev)
        kw, qw = weights(tb, lg)
        q = q_ref[0, :, h * RET_DK:(h + 1) * RET_DK]
        k = k_ref[0, :, h * RET_DK:(h + 1) * RET_DK]
        v = v_ref[0, :, h * RET_DV:(h + 1) * RET_DV]
        s = s_ref[h]
        o = _dot(q, s) * qw
        if not rev:
            o = o + _dot(_dot_nt(q, k) * dm_ref[h], v)
        o_ref[0, :, h * RET_DV:(h + 1) * RET_DV] = o.astype(o_ref.dtype)
        s_ref[h] = s * math.exp(lg * tb) + _dot_tn(k.astype(F32) * kw, v)


def _ret_decay_matrix(tb):
    rel = np.arange(tb)[:, None] - np.arange(tb)[None, :]
    out = np.zeros((RET_HEADS, tb, tb), np.float64)
    for h in range(RET_HEADS):
        fw = np.exp(_ret_log_gamma(h, False) * np.maximum(rel, 0))
        bw = np.exp(_ret_log_gamma(h, True) * np.maximum(-rel, 0))
        out[h] = np.where(rel > 0, fw, np.where(rel < 0, bw, 2.0))
    return jnp.asarray(out, F32)


def _ret_scan(q, k, v, kc, vc, *, rev, tb):
    b, l, _ = q.shape
    lc = kc.shape[1]
    nb = l // tb
    blk = (lambda j: nb - 1 - j) if rev else (lambda j: j)
    row = lambda w: pl.BlockSpec((1, tb, w), lambda i, j: (i, blk(j), 0))
    ctx = lambda w: pl.BlockSpec((1, lc, w), lambda i, j: (i, 0, 0))
    in_specs = [row(ODD_Q), row(ODD_Q), row(ODD_V), ctx(ODD_Q), ctx(ODD_V)]
    args = [q, k, v, kc, vc]
    if not rev:
        in_specs.append(_const_spec((RET_HEADS, tb, tb)))
        args.append(_ret_decay_matrix(tb))
    return pl.pallas_call(
        functools.partial(_ret_kernel, rev=rev, tb=tb),
        out_shape=jax.ShapeDtypeStruct((b, l, ODD_V), F32),
        grid=(b, nb),
        in_specs=in_specs,
        out_specs=row(ODD_V),
        scratch_shapes=[pltpu.VMEM((RET_HEADS, RET_DK, RET_DV), F32)],
        compiler_params=_cparams(("parallel", "arbitrary")),
    )(*args)


def _out_odd_kernel(of_ref, ob_ref, g_ref, x_ref, mod_ref, ng2_ref, wo_ref, wg_ref, wu_ref, wd_ref,
                    o_ref, acc_ref):
    o = of_ref[0] + ob_ref[0]
    parts = []
    for h in range(RET_HEADS):
        oh = o[:, h * RET_DV:(h + 1) * RET_DV]
        parts.append(oh * lax.rsqrt(jnp.mean(oh * oh, axis=-1, keepdims=True) + EPS))
    mix = jnp.concatenate(parts, axis=1) * _silu(g_ref[0])
    x1 = x_ref[0] + mod_ref[2:3, :] * _dot(mix, wo_ref[...])
    o_ref[0] = _swiglu_residual(x1, mod_ref, ng2_ref, wg_ref, wu_ref, wd_ref, acc_ref)


def _out_odd(o_fw, o_bw, g, x, mod4, layer, ng2, wo, wg, wu, wd, *, tm):
    b, l, d = x.shape
    nch = wg.shape[0]
    row = lambda w: pl.BlockSpec((1, tm, w), lambda i, j: (i, j, 0))
    return pl.pallas_call(
        _out_odd_kernel,
        out_shape=jax.ShapeDtypeStruct((b, l, d), F32),
        grid=(b, l // tm),
        in_specs=[row(ODD_V), row(ODD_V), row(ODD_V), row(d),
                  pl.BlockSpec((None, None, 6, d), lambda i, j: (layer, i, 0, 0)),
                  _const_spec((1, d)), _const_spec((ODD_V, d)),
                  _const_spec((nch, d, FFN_CHUNK)), _const_spec((nch, d, FFN_CHUNK)),
                  _const_spec((nch, FFN_CHUNK, d))],
        out_specs=row(d),
        scratch_shapes=[pltpu.VMEM((tm, d), F32)],
        compiler_params=_cparams(("parallel", "parallel")),
    )(o_fw, o_bw, g, x, mod4, ng2, wo, wg, wu, wd)


def _axial_tables(n_tok):
    t = np.arange(n_tok)
    n_freq = HEAD_DIM // 4
    inv = ROPE_BASE ** (-np.arange(n_freq, dtype=np.float64) / n_freq)
    ang = np.concatenate([(t // GRID_W)[:, None] * inv, (t % GRID_W)[:, None] * inv], axis=-1)
    cos, sin = np.cos(ang), np.sin(ang)
    cs = np.tile(np.concatenate([cos, cos], axis=-1), (1, LANES // HEAD_DIM))
    sn = np.tile(np.concatenate([-sin, sin], axis=-1), (1, LANES // HEAD_DIM))
    return jnp.asarray(cs, F32), jnp.asarray(sn, F32)


def _retention_tables(n_tok):
    theta = 1.0 / (RET_BASE ** np.linspace(0.0, 1.0, RET_DK // 2))
    ang = np.arange(n_tok, dtype=np.float64)[:, None] * theta
    return jnp.asarray(np.cos(ang), F32), jnp.asarray(np.sin(ang), F32)


def _block_diag_mean(width, head):
    idx = np.arange(width) // head
    return jnp.asarray((idx[:, None] == idx[None, :]) / head, MXU_DTYPE)


def _ffn_weights(w_in, w_out):
    d = w_in.shape[0]
    nch = FFN_HIDDEN // FFN_CHUNK
    wg = _mx(w_in[:, :FFN_HIDDEN]).reshape(d, nch, FFN_CHUNK).transpose(1, 0, 2)
    wu = _mx(w_in[:, FFN_HIDDEN:]).reshape(d, nch, FFN_CHUNK).transpose(1, 0, 2)
    wd = _mx(w_out).reshape(nch, FFN_CHUNK, d)
    return wg, wu, wd


def _pick(n, pref):
    while n % pref:
        pref //= 2
    return pref


def kernel(x, c, ctx, c_ctx, mod_w, mod_b, norm_g, ffn_w_in, ffn_w_out, even_w_in, even_w_out,
           attn_qk_norm_g, attn_sink, hgrn_out_norm_g, hgrn_lb, odd_w_in, odd_w_out):
    n_b, n_tok, d = x.shape
    n_ctx = ctx.shape[1]
    depth = mod_w.shape[0]
    assert depth == 2 and d == D_MODEL

    rows = -(-(n_b + 1) // 8) * 8
    cond = jnp.zeros((rows, d), F32).at[:n_b].set(c).at[n_b].set(c_ctx)
    mod4 = _modulation(cond, mod_w, mod_b).reshape(depth, rows, 6, d)
    lat_row = lambda i: i
    ctx_row = lambda i: n_b

    ng1, ng2 = norm_g[0, 0][None, :], norm_g[0, 1][None, :]
    w_in = _mx(even_w_in[0])
    gq = jnp.tile(attn_qk_norm_g[0, 0], ATTN_HEADS)[None, :]
    gk = jnp.tile(attn_qk_norm_g[0, 1], ATTN_KV_HEADS)[None, :]
    bdq, bdk = _block_diag_mean(A_Q, HEAD_DIM), _block_diag_mean(A_KV, HEAD_DIM)
    cs_a, sn_a = _axial_tables(n_tok)
    lower = jnp.cumsum(jax.nn.softmax(hgrn_lb.astype(F32), axis=0), axis=0)[0][None, :]
    og = hgrn_out_norm_g[0][None, :]
    wo = _mx(even_w_out[0])
    wg, wu, wd = _ffn_weights(ffn_w_in[0], ffn_w_out[0])

    tm_c = _pick(n_ctx, 256)
    tm_l = _pick(n_tok, 512)
    ones_c = jnp.ones((n_ctx, LANES), F32)
    qc, kc, vc, hgc = _proj_even(ctx, mod4, 0, ctx_row, ng1, w_in, gq, gk, bdq, bdk, ones_c, ones_c,
                                 rope=False, tm=tm_c)
    ql, kl, vl, hgl = _proj_even(x, mod4, 0, lat_row, ng1, w_in, gq, gk, bdq, bdk, cs_a, sn_a,
                                 rope=True, tm=tm_l)

    a_ctx = _attention(qc, None, None, kc, vc, attn_sink[0], window=False, tq=WINDOW)
    a_lat = _attention(ql, kl, vl, kc, vc, attn_sink[0], window=True, tq=WINDOW)

    zeros = jnp.zeros((n_b, HGRN_HEADS, HGRN_DK, HGRN_DK), F32)
    tb_c, tb_l = _pick(n_ctx, 256), _pick(n_tok, 256)
    o_fw_c, s_fw = _hgrn_scan(hgc, lower, zeros, rev=False, tb=tb_c)
    o_bw_c, s_bw = _hgrn_scan(hgc, lower, zeros, rev=True, tb=tb_c)
    o_fw, _ = _hgrn_scan(hgl, lower, s_fw, rev=False, tb=tb_l)
    o_bw, _ = _hgrn_scan(hgl, lower, s_bw, rev=True, tb=tb_l)

    ctx1 = _out_even(a_ctx, o_fw_c, o_bw_c, hgc, ctx, mod4, 0, ctx_row, og, ng2, wo, wg, wu, wd, tm=tm_c)
    x1 = _out_even(a_lat, o_fw, o_bw, hgl, x, mod4, 0, lat_row, og, ng2, wo, wg, wu, wd, tm=tm_l)

    ng1, ng2 = norm_g[1, 0][None, :], norm_g[1, 1][None, :]
    w_in = _mx(odd_w_in[0])
    cs_r, sn_r = _retention_tables(n_tok)
    wo = _mx(odd_w_out[0])
    wg, wu, wd = _ffn_weights(ffn_w_in[1], ffn_w_out[1])

    kc, vc = _proj_ctx_kv(ctx1, mod4, 1, n_b, ng1, w_in[:, ODD_Q:2 * ODD_Q + ODD_V])
    tm_p = _pick(n_tok, 512)
    q, k, v, g = _proj_odd(x1, mod4, 1, ng1, w_in, cs_r, sn_r, tm=tm_p)
    tb_r = _pick(n_tok, 256)
    o_fw = _ret_scan(q, k, v, kc, vc, rev=False, tb=tb_r)
    o_bw = _ret_scan(q, k, v, kc, vc, rev=True, tb=tb_r)
    return _out_odd(o_fw, o_bw, g, x1, mod4, 1, ng2, wo, wg, wu, wd, tm=_pick(n_tok, 256))
```

```python
import functools
import math

import numpy as np
import jax
import jax.numpy as jnp
from jax import lax
from jax.experimental import pallas as pl
from jax.experimental.pallas import tpu as pltpu

F32 = jnp.float32
MXU_DTYPE = jnp.bfloat16

D_MODEL = 1024
GRID_W = 64
HEAD_DIM = 64
ATTN_HEADS = 8
ATTN_KV_HEADS = 2
WINDOW = 128
ROPE_BASE = 10000.0
HGRN_HEADS = 4
HGRN_DK = 128
HGRN_WIDTH = HGRN_HEADS * HGRN_DK
GLA_CHUNK = 64
RET_HEADS = 4
RET_DK = 256
RET_DV = 512
RET_BASE = 10000.0
FFN_HIDDEN = 2816
A_Q = ATTN_HEADS * HEAD_DIM
A_KV = ATTN_KV_HEADS * HEAD_DIM
EVEN_IN = A_Q + 2 * A_KV + 5 * HGRN_WIDTH
HG_OFF = A_Q + 2 * A_KV
ODD_Q = RET_HEADS * RET_DK
ODD_V = RET_HEADS * RET_DV
ODD_IN = 2 * ODD_Q + 2 * ODD_V
EPS = 1e-6
NEG = -1e30

LANES = 128
FFN_CHUNK = 256
VMEM_LIMIT = 56 * 1024 * 1024


def _mx(a):
    return a.astype(MXU_DTYPE)


def _dot(a, b):
    return jnp.dot(_mx(a), _mx(b), preferred_element_type=F32)


def _dot_nt(a, b):
    return lax.dot_general(_mx(a), _mx(b), (((1,), (1,)), ((), ())), preferred_element_type=F32)


def _dot_tn(a, b):
    return lax.dot_general(_mx(a), _mx(b), (((0,), (0,)), ((), ())), preferred_element_type=F32)


def _split(x):
    hi = _mx(x)
    lo = _mx(x - hi.astype(F32))
    return hi, lo


def _sigmoid(x):
    return 1.0 / (1.0 + jnp.exp(-x))


def _silu(x):
    return x * _sigmoid(x)


def _norm_mod(x, g, shift, scale):
    y = x * lax.rsqrt(jnp.mean(x * x, axis=-1, keepdims=True) + EPS) * g
    return y * (1.0 + scale) + shift


def _cparams(sem):
    return pltpu.CompilerParams(dimension_semantics=sem, vmem_limit_bytes=VMEM_LIMIT)


def _const_spec(shape):
    nd = len(shape)
    return pl.BlockSpec(shape, lambda *_: (0,) * nd, pipeline_mode=pl.Buffered(1))


def _mod_kernel(c_ref, w_ref, b_ref, o_ref):
    s = _silu(c_ref[...])
    o_ref[0] = jnp.dot(s, w_ref[0], preferred_element_type=F32,
                       precision=lax.Precision.HIGHEST) + b_ref[0]


def _modulation(cond, mod_w, mod_b):
    depth, d, n = mod_w.shape
    rows = cond.shape[0]
    tn = 1536
    return pl.pallas_call(
        _mod_kernel,
        out_shape=jax.ShapeDtypeStruct((depth, rows, n), F32),
        grid=(depth, n // tn),
        in_specs=[pl.BlockSpec((rows, d), lambda l, j: (0, 0)),
                  pl.BlockSpec((1, d, tn), lambda l, j: (l, 0, j)),
                  pl.BlockSpec((1, 1, tn), lambda l, j: (l, 0, j))],
        out_specs=pl.BlockSpec((1, rows, tn), lambda l, j: (l, 0, j)),
        compiler_params=_cparams(("arbitrary", "arbitrary")),
    )(cond, mod_w, mod_b.reshape(depth, 1, n))


def _head_rms(u, bd, gain):
    hi, lo = _split(u * u)
    ms = jnp.dot(hi, bd, preferred_element_type=F32) + jnp.dot(lo, bd, preferred_element_type=F32)
    return u * lax.rsqrt(ms + EPS) * gain


def _rope_half_swap(u):
    lane = lax.broadcasted_iota(jnp.int32, u.shape, 1)
    first = (lane % HEAD_DIM) < (HEAD_DIM // 2)
    return jnp.where(first, pltpu.roll(u, LANES - HEAD_DIM // 2, 1), pltpu.roll(u, HEAD_DIM // 2, 1))


def _dup_halves(u):
    lane = lax.broadcasted_iota(jnp.int32, u.shape, 1)
    low = lane < HEAD_DIM
    swapped = pltpu.roll(u, HEAD_DIM, 1)
    return jnp.concatenate([jnp.where(low, u, swapped), jnp.where(low, swapped, u)], axis=1)


def _proj_even_kernel(x_ref, mod_ref, ng_ref, w_ref, gq_ref, gk_ref, bdq_ref, bdk_ref, cs_ref, sn_ref,
                      q_ref, k_ref, v_ref, hg_ref, *, rope):
    h = _mx(_norm_mod(x_ref[0], ng_ref[...], mod_ref[0:1, :], mod_ref[1:2, :]))
    qkv = jnp.dot(h, w_ref[:, :HG_OFF], preferred_element_type=F32)
    q = _head_rms(qkv[:, :A_Q], bdq_ref[...], gq_ref[...])
    k = _head_rms(qkv[:, A_Q:A_Q + A_KV], bdk_ref[...], gk_ref[...])
    v = qkv[:, A_Q + A_KV:HG_OFF]
    if rope:
        cs, sn = cs_ref[...], sn_ref[...]
        q = jnp.concatenate(
            [q[:, c * LANES:(c + 1) * LANES] * cs + _rope_half_swap(q[:, c * LANES:(c + 1) * LANES]) * sn
             for c in range(A_Q // LANES)], axis=1)
        k = k * cs + _rope_half_swap(k) * sn
    q_ref[0] = (q * HEAD_DIM ** -0.5).astype(q_ref.dtype)
    k_ref[0] = _dup_halves(k).astype(k_ref.dtype)
    v_ref[0] = _dup_halves(v).astype(v_ref.dtype)
    hg_ref[0] = jnp.dot(h, w_ref[:, HG_OFF:], preferred_element_type=F32)


def _proj_even(x, mod4, layer, row_of_batch, ng, w, gq, gk, bdq, bdk, cs, sn, *, rope, tm):
    b, l, d = x.shape
    nblk = l // tm
    kern = functools.partial(_proj_even_kernel, rope=rope)
    tab_spec = pl.BlockSpec((tm, LANES), lambda i, j: (j, 0))
    return pl.pallas_call(
        kern,
        out_shape=(jax.ShapeDtypeStruct((b, l, A_Q), MXU_DTYPE),
                   jax.ShapeDtypeStruct((b, l, 2 * A_KV), MXU_DTYPE),
                   jax.ShapeDtypeStruct((b, l, 2 * A_KV), MXU_DTYPE),
                   jax.ShapeDtypeStruct((b, l, 5 * HGRN_WIDTH), F32)),
        grid=(b, nblk),
        in_specs=[pl.BlockSpec((1, tm, d), lambda i, j: (i, j, 0)),
                  pl.BlockSpec((None, None, 6, d), lambda i, j: (layer, row_of_batch(i), 0, 0)),
                  _const_spec((1, d)),
                  _const_spec((d, EVEN_IN)),
                  _const_spec((1, A_Q)), _const_spec((1, A_KV)),
                  _const_spec((A_Q, A_Q)), _const_spec((A_KV, A_KV)),
                  tab_spec, tab_spec],
        out_specs=(pl.BlockSpec((1, tm, A_Q), lambda i, j: (i, j, 0)),
                   pl.BlockSpec((1, tm, 2 * A_KV), lambda i, j: (i, j, 0)),
                   pl.BlockSpec((1, tm, 2 * A_KV), lambda i, j: (i, j, 0)),
                   pl.BlockSpec((1, tm, 5 * HGRN_WIDTH), lambda i, j: (i, j, 0))),
        compiler_params=_cparams(("parallel", "parallel")),
    )(x, mod4, ng, w, gq, gk, bdq, bdk, cs, sn)


def _attn_kernel(*refs, window, tq, seq):
    if window:
        (q_ref, kp_ref, kc_ref, kn_ref, vp_ref, vc_ref, vn_ref, kx_ref, vx_ref, sink_ref, o_ref) = refs
    else:
        (q_ref, kx_ref, vx_ref, sink_ref, o_ref) = refs
    i = pl.program_id(1)
    n_ctx = kx_ref.shape[1]
    if window:
        span = tq + 2 * WINDOW
        qpos = i * tq + lax.broadcasted_iota(jnp.int32, (tq, span), 0)
        kpos = i * tq - WINDOW + lax.broadcasted_iota(jnp.int32, (tq, span), 1)
        ok = (jnp.abs(kpos - qpos) <= WINDOW) & (kpos >= 0) & (kpos < seq)
        bias = jnp.concatenate([jnp.where(ok, 0.0, NEG), jnp.zeros((tq, n_ctx), F32)], axis=1)
        bias = jnp.concatenate([bias] * 4, axis=0)
    lane = lax.broadcasted_iota(jnp.int32, (2 * tq, LANES), 1)
    low = lane < HEAD_DIM
    q = q_ref[0]
    for g in range(ATTN_KV_HEADS):
        cols = slice(g * LANES, (g + 1) * LANES)
        if window:
            kk = jnp.concatenate([kp_ref[0, :, cols], kc_ref[0, :, cols], kn_ref[0, :, cols],
                                  kx_ref[0, :, cols]], axis=0)
            vv = jnp.concatenate([vp_ref[0, :, cols], vc_ref[0, :, cols], vn_ref[0, :, cols],
                                  vx_ref[0, :, cols]], axis=0)
        else:
            kk, vv = kx_ref[0, :, cols], vx_ref[0, :, cols]
        q2 = jnp.concatenate([q[:, (2 * g) * LANES:(2 * g + 1) * LANES],
                              q[:, (2 * g + 1) * LANES:(2 * g + 2) * LANES]], axis=0)
        zero = jnp.zeros_like(q2)
        q4 = jnp.concatenate([jnp.where(low, q2, zero), jnp.where(low, zero, q2)], axis=0)
        s = _dot_nt(q4, kk)
        if window:
            s = s + bias
        sink = sink_ref[g]
        m = jnp.maximum(jnp.max(s, axis=-1, keepdims=True), sink)
        p = jnp.exp(s - m)
        denom = jnp.sum(p, axis=-1, keepdims=True) + jnp.exp(sink - m)
        o = _dot(p, vv) * (1.0 / denom)
        o2 = jnp.where(low, o[:2 * tq], o[2 * tq:])
        o_ref[0, :, (2 * g) * LANES:(2 * g + 1) * LANES] = o2[:tq].astype(o_ref.dtype)
        o_ref[0, :, (2 * g + 1) * LANES:(2 * g + 2) * LANES] = o2[tq:].astype(o_ref.dtype)


def _sink_rows(sink, tq):
    s = sink.astype(F32).reshape(ATTN_KV_HEADS, 4)
    order = jnp.array([0, 2, 1, 3])
    return jnp.repeat(s[:, order], tq, axis=1)[:, :, None]


def _attention(q, k, v, kx, vx, sink, *, window, tq):
    b, l, _ = q.shape
    n_ctx = kx.shape[1]
    nq = l // tq
    kern = functools.partial(_attn_kernel, window=window, tq=tq, seq=l)
    qspec = pl.BlockSpec((1, tq, A_Q), lambda i, j: (i, j, 0))
    xspec = pl.BlockSpec((1, n_ctx, 2 * A_KV), lambda i, j: (i, 0, 0))
    sspec = _const_spec((ATTN_KV_HEADS, 4 * tq, 1))
    if window:
        r = tq // WINDOW
        prev = pl.BlockSpec((1, WINDOW, 2 * A_KV), lambda i, j: (i, jnp.maximum(j * r - 1, 0), 0))
        cur = pl.BlockSpec((1, tq, 2 * A_KV), lambda i, j: (i, j, 0))
        nxt = pl.BlockSpec((1, WINDOW, 2 * A_KV), lambda i, j: (i, jnp.minimum((j + 1) * r, nq * r - 1), 0))
        in_specs = [qspec, prev, cur, nxt, prev, cur, nxt, xspec, xspec, sspec]
        args = (q, k, k, k, v, v, v, kx, vx, _sink_rows(sink, tq))
    else:
        in_specs = [qspec, xspec, xspec, sspec]
        args = (q, kx, vx, _sink_rows(sink, tq))
    return pl.pallas_call(
        kern,
        out_shape=jax.ShapeDtypeStruct((b, l, A_Q), MXU_DTYPE),
        grid=(b, nq),
        in_specs=in_specs,
        out_specs=pl.BlockSpec((1, tq, A_Q), lambda i, j: (i, j, 0)),
        compiler_params=_cparams(("parallel", "parallel")),
    )(*args)


def _hgrn_kernel(*refs, rev, tb):
    if rev:
        q_ref, f_ref, i_ref, lb_ref, s0_ref, o_ref, sfin_ref, st_ref = refs
    else:
        q_ref, f_ref, i_ref, lb_ref, s0_ref, ob_ref, g_ref, og_ref, o_ref, sfin_ref, st_ref = refs
    j = pl.program_id(1)
    nc = tb // GLA_CHUNK
    c_len = GLA_CHUNK
    width = HGRN_WIDTH

    @pl.when(j == 0)
    def _():
        st_ref[...] = s0_ref[0]

    lb = lb_ref[...]
    qs = _silu(q_ref[0])
    f = lb + (1.0 - lb) * _sigmoid(f_ref[0])
    kf = 1.0 - f
    lf = jnp.log(f)
    iv = _mx(i_ref[0])

    r = lax.broadcasted_iota(jnp.int32, (tb, tb), 0)
    c = lax.broadcasted_iota(jnp.int32, (tb, tb), 1)
    same = (r // c_len) == (c // c_len)
    lower = same & (c <= r)
    tri = _mx(jnp.where(lower, 1.0, 0.0))
    hi, lo = _split(lf)
    cum = (jnp.dot(tri, hi, preferred_element_type=F32)
           + jnp.dot(tri, lo, preferred_element_type=F32)).reshape(nc, c_len, width)
    if rev:
        w3 = cum[:, c_len - 1:c_len, :] - cum + lf.reshape(nc, c_len, width)
        edge = w3[:, 0:1, :]
        keep = same & (c >= r)
    else:
        w3 = cum
        edge = cum[:, c_len - 1:c_len, :]
        keep = lower
    mid = w3[:, c_len // 2:c_len // 2 + 1, :]
    qs3 = qs.reshape(nc, c_len, width)
    kf3 = kf.reshape(nc, c_len, width)
    qa = _mx((qs3 * jnp.exp(w3 - mid)).reshape(tb, width))
    kb = _mx((kf3 * jnp.exp(mid - w3)).reshape(tb, width))
    qe = _mx((qs3 * jnp.exp(w3)).reshape(tb, width))
    ke = _mx((kf3 * jnp.exp(edge - w3)).reshape(tb, width))
    dec = jnp.exp(edge)

    order = range(nc - 1, -1, -1) if rev else range(nc)
    for h in range(HGRN_HEADS):
        hs = slice(h * HGRN_DK, (h + 1) * HGRN_DK)
        s = jnp.where(keep, _dot_nt(qa[:, hs], kb[:, hs]), 0.0)
        o_intra = _dot(s, iv[:, hs])
        st = st_ref[h]
        pieces = [None] * nc
        for ci in order:
            rows = slice(ci * c_len, (ci + 1) * c_len)
            pieces[ci] = o_intra[rows] + _dot_nt(qe[rows, hs], st)
            st = st * dec[ci, :, hs] + _dot_tn(iv[rows, hs], ke[rows, hs])
        st_ref[h] = st
        oh = jnp.concatenate(pieces, axis=0)
        if not rev:
            oh = oh + ob_ref[0, :, hs]
            oh = oh * lax.rsqrt(jnp.mean(oh * oh, axis=-1, keepdims=True) + EPS) * og_ref[...]
            oh = oh * _silu(g_ref[0, :, hs])
        o_ref[0, :, hs] = oh.astype(o_ref.dtype)

    @pl.when(j == pl.num_programs(1) - 1)
    def _():
        sfin_ref[0] = st_ref[...]


def _hgrn_scan(hg, lb, s0, o_bw=None, og=None, *, tb):
    rev = o_bw is None
    b, l, _ = hg.shape
    nb = l // tb
    blk = (lambda j: nb - 1 - j) if rev else (lambda j: j)
    col = lambda cidx: pl.BlockSpec((1, tb, HGRN_WIDTH), lambda i, j: (i, blk(j), cidx))
    st_shape = (HGRN_HEADS, HGRN_DK, HGRN_DK)
    st_spec = pl.BlockSpec((1,) + st_shape, lambda i, j: (i, 0, 0, 0))
    in_specs = [col(0), col(2 if rev else 1), col(3), _const_spec((1, HGRN_WIDTH)), st_spec]
    args = [hg, hg, hg, lb, s0]
    if not rev:
        in_specs += [col(0), col(4), _const_spec((1, HGRN_DK))]
        args += [o_bw, hg, og]
    return pl.pallas_call(
        functools.partial(_hgrn_kernel, rev=rev, tb=tb),
        out_shape=(jax.ShapeDtypeStruct((b, l, HGRN_WIDTH), F32 if rev else MXU_DTYPE),
                   jax.ShapeDtypeStruct((b,) + st_shape, F32)),
        grid=(b, nb),
        in_specs=in_specs,
        out_specs=(col(0), st_spec),
        scratch_shapes=[pltpu.VMEM(st_shape, F32)],
        compiler_params=_cparams(("parallel", "arbitrary")),
    )(*args)


def _swiglu_residual(x1, mod_ref, ng2_ref, wg_ref, wu_ref, wd_ref):
    h2 = _mx(_norm_mod(x1, ng2_ref[...], mod_ref[3:4, :], mod_ref[4:5, :]))
    acc = None
    for ci in range(wg_ref.shape[0]):
        gate = jnp.dot(h2, wg_ref[ci], preferred_element_type=F32)
        up = jnp.dot(h2, wu_ref[ci], preferred_element_type=F32)
        part = _dot(_silu(gate) * up, wd_ref[ci])
        acc = part if acc is None else acc + part
    return x1 + mod_ref[5:6, :] * acc


def _out_kernel(*refs, n_mix):
    mix_refs = refs[:n_mix]
    x_ref, mod_ref, ng2_ref, wo_ref, wg_ref, wu_ref, wd_ref, o_ref = refs[n_mix:]
    y, off = None, 0
    for m_ref in mix_refs:
        w = m_ref.shape[-1]
        part = jnp.dot(m_ref[0], wo_ref[off:off + w, :], preferred_element_type=F32)
        y = part if y is None else y + part
        off += w
    x1 = x_ref[0] + mod_ref[2:3, :] * y
    o_ref[0] = _swiglu_residual(x1, mod_ref, ng2_ref, wg_ref, wu_ref, wd_ref)


def _out_layer(mixes, x, mod4, layer, row_of_batch, ng2, wo, wg, wu, wd, *, tm):
    b, l, d = x.shape
    nch = wg.shape[0]
    row = lambda w: pl.BlockSpec((1, tm, w), lambda i, j: (i, j, 0))
    return pl.pallas_call(
        functools.partial(_out_kernel, n_mix=len(mixes)),
        out_shape=jax.ShapeDtypeStruct((b, l, d), F32),
        grid=(b, l // tm),
        in_specs=[row(m.shape[-1]) for m in mixes] + [
            row(d),
            pl.BlockSpec((None, None, 6, d), lambda i, j: (layer, row_of_batch(i), 0, 0)),
            _const_spec((1, d)), _const_spec(wo.shape),
            _const_spec((nch, d, FFN_CHUNK)), _const_spec((nch, d, FFN_CHUNK)),
            _const_spec((nch, FFN_CHUNK, d))],
        out_specs=row(d),
        compiler_params=_cparams(("parallel", "parallel")),
    )(*mixes, x, mod4, ng2, wo, wg, wu, wd)


def _rope_ret(u, cs, sn):
    parts = []
    for h in range(RET_HEADS):
        x1 = u[:, h * RET_DK:h * RET_DK + LANES]
        x2 = u[:, h * RET_DK + LANES:(h + 1) * RET_DK]
        parts += [x1 * cs - x2 * sn, x2 * cs + x1 * sn]
    return jnp.concatenate(parts, axis=1)


def _proj_odd_kernel(x_ref, mod_ref, ng_ref, w_ref, cs_ref, sn_ref, q_ref, k_ref, v_ref, g_ref):
    h = _mx(_norm_mod(x_ref[0], ng_ref[...], mod_ref[0:1, :], mod_ref[1:2, :]))
    cs, sn = cs_ref[...], sn_ref[...]
    q = jnp.dot(h, w_ref[:, :ODD_Q], preferred_element_type=F32)
    q_ref[0] = _rope_ret(q, cs, sn).astype(q_ref.dtype)
    k = jnp.dot(h, w_ref[:, ODD_Q:2 * ODD_Q], preferred_element_type=F32)
    k_ref[0] = (_rope_ret(k, cs, sn) * RET_DK ** -0.5).astype(k_ref.dtype)
    v_ref[0] = jnp.dot(h, w_ref[:, 2 * ODD_Q:2 * ODD_Q + ODD_V], preferred_element_type=F32).astype(v_ref.dtype)
    g_ref[0] = jnp.dot(h, w_ref[:, 2 * ODD_Q + ODD_V:], preferred_element_type=F32).astype(g_ref.dtype)


def _proj_odd(x, mod4, layer, ng, w, cs, sn, *, tm):
    b, l, d = x.shape
    row = lambda wd: pl.BlockSpec((1, tm, wd), lambda i, j: (i, j, 0))
    tab = pl.BlockSpec((tm, LANES), lambda i, j: (j, 0))
    return pl.pallas_call(
        _proj_odd_kernel,
        out_shape=(jax.ShapeDtypeStruct((b, l, ODD_Q), MXU_DTYPE),
                   jax.ShapeDtypeStruct((b, l, ODD_Q), MXU_DTYPE),
                   jax.ShapeDtypeStruct((b, l, ODD_V), MXU_DTYPE),
                   jax.ShapeDtypeStruct((b, l, ODD_V), MXU_DTYPE)),
        grid=(b, l // tm),
        in_specs=[row(d),
                  pl.BlockSpec((None, None, 6, d), lambda i, j: (layer, i, 0, 0)),
                  _const_spec((1, d)), _const_spec((d, ODD_IN)), tab, tab],
        out_specs=(row(ODD_Q), row(ODD_Q), row(ODD_V), row(ODD_V)),
        compiler_params=_cparams(("parallel", "parallel")),
    )(x, mod4, ng, w, cs, sn)


def _proj_ctx_kv_kernel(x_ref, mod_ref, ng_ref, w_ref, k_ref, v_ref):
    h = _mx(_norm_mod(x_ref[0], ng_ref[...], mod_ref[0:1, :], mod_ref[1:2, :]))
    k = jnp.dot(h, w_ref[:, :ODD_Q], preferred_element_type=F32)
    k_ref[0] = (k * RET_DK ** -0.5).astype(k_ref.dtype)
    v_ref[0] = jnp.dot(h, w_ref[:, ODD_Q:], preferred_element_type=F32).astype(v_ref.dtype)


def _proj_ctx_kv(ctx, mod4, layer, ctx_row, ng, w_kv):
    b, lc, d = ctx.shape
    return pl.pallas_call(
        _proj_ctx_kv_kernel,
        out_shape=(jax.ShapeDtypeStruct((b, lc, ODD_Q), MXU_DTYPE),
                   jax.ShapeDtypeStruct((b, lc, ODD_V), MXU_DTYPE)),
        grid=(b,),
        in_specs=[pl.BlockSpec((1, lc, d), lambda i: (i, 0, 0)),
                  pl.BlockSpec((None, None, 6, d), lambda i: (layer, ctx_row, 0, 0)),
                  _const_spec((1, d)), _const_spec((d, ODD_Q + ODD_V))],
        out_specs=(pl.BlockSpec((1, lc, ODD_Q), lambda i: (i, 0, 0)),
                   pl.BlockSpec((1, lc, ODD_V), lambda i: (i, 0, 0))),
        compiler_params=_cparams(("parallel",)),
    )(ctx, mod4, ng, w_kv)


def _ret_log_gamma(h, rev):
    hh = RET_HEADS - 1 - h if rev else h
    return math.log(1.0 - 2.0 ** (-5.0 - hh))


def _ret_kernel(*refs, rev, tb):
    if rev:
        q_ref, k_ref, v_ref, kc_ref, vc_ref, o_ref, s_ref = refs
    else:
        q_ref, k_ref, v_ref, kc_ref, vc_ref, dm_ref, ob_ref, g_ref, o_ref, s_ref = refs
    j = pl.program_id(1)
    lc = kc_ref.shape[1]

    def weights(n, lg):
        pos = lax.broadcasted_iota(jnp.int32, (n, 1), 0).astype(F32)
        if rev:
            return jnp.exp(lg * pos), jnp.exp(lg * (n - pos))
        return jnp.exp(lg * (n - 1.0 - pos)), jnp.exp(lg * (pos + 1.0))

    @pl.when(j == 0)
    def _():
        for h in range(RET_HEADS):
            kw, _ = weights(lc, _ret_log_gamma(h, rev))
            kc = kc_ref[0, :, h * RET_DK:(h + 1) * RET_DK].astype(F32) * kw
            s_ref[h] = _dot_tn(kc, vc_ref[0, :, h * RET_DV:(h + 1) * RET_DV])

    for h in range(RET_HEADS):
        lg = _ret_log_gamma(h, rev)
        kw, qw = weights(tb, lg)
        q = q_ref[0, :, h * RET_DK:(h + 1) * RET_DK]
        k = k_ref[0, :, h * RET_DK:(h + 1) * RET_DK]
        v = v_ref[0, :, h * RET_DV:(h + 1) * RET_DV]
        s = s_ref[h]
        o = _dot(q, s) * qw
        if not rev:
            vs = slice(h * RET_DV, (h + 1) * RET_DV)
            o = o + _dot(_dot_nt(q, k) * dm_ref[h], v) + ob_ref[0, :, vs].astype(F32)
            o = o * lax.rsqrt(jnp.mean(o * o, axis=-1, keepdims=True) + EPS)
            o = o * _silu(g_ref[0, :, vs].astype(F32))
        o_ref[0, :, h * RET_DV:(h + 1) * RET_DV] = o.astype(o_ref.dtype)
        s_ref[h] = s * math.exp(lg * tb) + _dot_tn(k.astype(F32) * kw, v)


def _ret_decay_matrix(tb):
    rel = np.arange(tb)[:, None] - np.arange(tb)[None, :]
    out = np.zeros((RET_HEADS, tb, tb), np.float64)
    for h in range(RET_HEADS):
        fw = np.exp(_ret_log_gamma(h, False) * np.maximum(rel, 0))
        bw = np.exp(_ret_log_gamma(h, True) * np.maximum(-rel, 0))
        out[h] = np.where(rel > 0, fw, np.where(rel < 0, bw, 2.0))
    return jnp.asarray(out, F32)


def _ret_scan(q, k, v, kc, vc, o_bw=None, g=None, *, tb):
    rev = o_bw is None
    b, l, _ = q.shape
    lc = kc.shape[1]
    nb = l // tb
    blk = (lambda j: nb - 1 - j) if rev else (lambda j: j)
    row = lambda w: pl.BlockSpec((1, tb, w), lambda i, j: (i, blk(j), 0))
    ctx = lambda w: pl.BlockSpec((1, lc, w), lambda i, j: (i, 0, 0))
    in_specs = [row(ODD_Q), row(ODD_Q), row(ODD_V), ctx(ODD_Q), ctx(ODD_V)]
    args = [q, k, v, kc, vc]
    if not rev:
        in_specs += [_const_spec((RET_HEADS, tb, tb)), row(ODD_V), row(ODD_V)]
        args += [_ret_decay_matrix(tb), o_bw, g]
    return pl.pallas_call(
        functools.partial(_ret_kernel, rev=rev, tb=tb),
        out_shape=jax.ShapeDtypeStruct((b, l, ODD_V), MXU_DTYPE),
        grid=(b, nb),
        in_specs=in_specs,
        out_specs=row(ODD_V),
        scratch_shapes=[pltpu.VMEM((RET_HEADS, RET_DK, RET_DV), F32)],
        compiler_params=_cparams(("parallel", "arbitrary")),
    )(*args)


def _axial_tables(n_tok):
    t = np.arange(n_tok)
    n_freq = HEAD_DIM // 4
    inv = ROPE_BASE ** (-np.arange(n_freq, dtype=np.float64) / n_freq)
    ang = np.concatenate([(t // GRID_W)[:, None] * inv, (t % GRID_W)[:, None] * inv], axis=-1)
    cos, sin = np.cos(ang), np.sin(ang)
    cs = np.tile(np.concatenate([cos, cos], axis=-1), (1, LANES // HEAD_DIM))
    sn = np.tile(np.concatenate([-sin, sin], axis=-1), (1, LANES // HEAD_DIM))
    return jnp.asarray(cs, F32), jnp.asarray(sn, F32)


def _retention_tables(n_tok):
    theta = 1.0 / (RET_BASE ** np.linspace(0.0, 1.0, RET_DK // 2))
    ang = np.arange(n_tok, dtype=np.float64)[:, None] * theta
    return jnp.asarray(np.cos(ang), F32), jnp.asarray(np.sin(ang), F32)


def _block_diag_mean(width, head):
    idx = np.arange(width) // head
    return jnp.asarray((idx[:, None] == idx[None, :]) / head, MXU_DTYPE)


def _ffn_weights(w_in, w_out):
    d = w_in.shape[0]
    nch = FFN_HIDDEN // FFN_CHUNK
    wg = _mx(w_in[:, :FFN_HIDDEN]).reshape(d, nch, FFN_CHUNK).transpose(1, 0, 2)
    wu = _mx(w_in[:, FFN_HIDDEN:]).reshape(d, nch, FFN_CHUNK).transpose(1, 0, 2)
    wd = _mx(w_out).reshape(nch, FFN_CHUNK, d)
    return wg, wu, wd


def _pick(n, pref):
    while n % pref:
        pref //= 2
    return pref


def kernel(x, c, ctx, c_ctx, mod_w, mod_b, norm_g, ffn_w_in, ffn_w_out, even_w_in, even_w_out,
           attn_qk_norm_g, attn_sink, hgrn_out_norm_g, hgrn_lb, odd_w_in, odd_w_out):
    n_b, n_tok, d = x.shape
    n_ctx = ctx.shape[1]
    depth = mod_w.shape[0]
    assert depth == 2 and d == D_MODEL

    rows = -(-(n_b + 1) // 8) * 8
    cond = jnp.zeros((rows, d), F32).at[:n_b].set(c).at[n_b].set(c_ctx)
    mod4 = _modulation(cond, mod_w, mod_b).reshape(depth, rows, 6, d)
    lat_row = lambda i: i
    ctx_row = lambda i: n_b

    ng1, ng2 = norm_g[0, 0][None, :], norm_g[0, 1][None, :]
    w_in = _mx(even_w_in[0])
    gq = jnp.tile(attn_qk_norm_g[0, 0], ATTN_HEADS)[None, :]
    gk = jnp.tile(attn_qk_norm_g[0, 1], ATTN_KV_HEADS)[None, :]
    bdq, bdk = _block_diag_mean(A_Q, HEAD_DIM), _block_diag_mean(A_KV, HEAD_DIM)
    cs_a, sn_a = _axial_tables(n_tok)
    lower = jnp.cumsum(jax.nn.softmax(hgrn_lb.astype(F32), axis=0), axis=0)[0][None, :]
    og = hgrn_out_norm_g[0][None, :]
    wo = _mx(even_w_out[0])
    wg, wu, wd = _ffn_weights(ffn_w_in[0], ffn_w_out[0])

    tm_c = _pick(n_ctx, 256)
    tm_l = _pick(n_tok, 512)
    ones_c = jnp.ones((n_ctx, LANES), F32)
    qc, kc, vc, hgc = _proj_even(ctx, mod4, 0, ctx_row, ng1, w_in, gq, gk, bdq, bdk, ones_c, ones_c,
                                 rope=False, tm=tm_c)
    ql, kl, vl, hgl = _proj_even(x, mod4, 0, lat_row, ng1, w_in, gq, gk, bdq, bdk, cs_a, sn_a,
                                 rope=True, tm=tm_l)

    a_ctx = _attention(qc, None, None, kc, vc, attn_sink[0], window=False, tq=WINDOW)
    a_lat = _attention(ql, kl, vl, kc, vc, attn_sink[0], window=True, tq=WINDOW)

    zeros = jnp.zeros((n_b, HGRN_HEADS, HGRN_DK, HGRN_DK), F32)
    tb_c, tb_l = _pick(n_ctx, 256), _pick(n_tok, 256)
    o_bw_c, s_bw = _hgrn_scan(hgc, lower, zeros, tb=tb_c)
    b_ctx, s_fw = _hgrn_scan(hgc, lower, zeros, o_bw_c, og, tb=tb_c)
    o_bw, _ = _hgrn_scan(hgl, lower, s_bw, tb=tb_l)
    b_lat, _ = _hgrn_scan(hgl, lower, s_fw, o_bw, og, tb=tb_l)

    ctx1 = _out_layer([a_ctx, b_ctx], ctx, mod4, 0, ctx_row, ng2, wo, wg, wu, wd, tm=tm_c)
    x1 = _out_layer([a_lat, b_lat], x, mod4, 0, lat_row, ng2, wo, wg, wu, wd, tm=tm_l)

    ng1, ng2 = norm_g[1, 0][None, :], norm_g[1, 1][None, :]
    w_in = _mx(odd_w_in[0])
    cs_r, sn_r = _retention_tables(n_tok)
    wo = _mx(odd_w_out[0])
    wg, wu, wd = _ffn_weights(ffn_w_in[1], ffn_w_out[1])

    kc, vc = _proj_ctx_kv(ctx1, mod4, 1, n_b, ng1, w_in[:, ODD_Q:2 * ODD_Q + ODD_V])
    tm_p = _pick(n_tok, 512)
    q, k, v, g = _proj_odd(x1, mod4, 1, ng1, w_in, cs_r, sn_r, tm=tm_p)
    tb_r = _pick(n_tok, 256)
    o_bw = _ret_scan(q, k, v, kc, vc, tb=tb_r)
    mix = _ret_scan(q, k, v, kc, vc, o_bw, g, tb=tb_r)
    return _out_layer([mix], x1, mod4, 1, lat_row, ng2, wo, wg, wu, wd, tm=_pick(n_tok, 512))
```

```python
import functools
import math

import numpy as np
import jax
import jax.numpy as jnp
from jax import lax
from jax.experimental import pallas as pl
from jax.experimental.pallas import tpu as pltpu

F32 = jnp.float32
MXU_DTYPE = jnp.bfloat16

D_MODEL = 1024
GRID_W = 64
HEAD_DIM = 64
ATTN_HEADS = 8
ATTN_KV_HEADS = 2
WINDOW = 128
ROPE_BASE = 10000.0
HGRN_HEADS = 4
HGRN_DK = 128
HGRN_WIDTH = HGRN_HEADS * HGRN_DK
GLA_CHUNK = 64
RET_HEADS = 4
RET_DK = 256
RET_DV = 512
RET_BASE = 10000.0
FFN_HIDDEN = 2816
A_Q = ATTN_HEADS * HEAD_DIM
A_KV = ATTN_KV_HEADS * HEAD_DIM
EVEN_IN = A_Q + 2 * A_KV + 5 * HGRN_WIDTH
HG_OFF = A_Q + 2 * A_KV
ODD_Q = RET_HEADS * RET_DK
ODD_V = RET_HEADS * RET_DV
ODD_IN = 2 * ODD_Q + 2 * ODD_V
EPS = 1e-6
NEG = -1e30
LOG2E = 1.4426950408889634

LANES = 128
FFN_CHUNK = 256
VMEM_LIMIT = 56 * 1024 * 1024


def _mx(a):
    return a.astype(MXU_DTYPE)


def _dot(a, b):
    return jnp.dot(_mx(a), _mx(b), preferred_element_type=F32)


def _dot_nt(a, b):
    return lax.dot_general(_mx(a), _mx(b), (((1,), (1,)), ((), ())), preferred_element_type=F32)


def _dot_tn(a, b):
    return lax.dot_general(_mx(a), _mx(b), (((0,), (0,)), ((), ())), preferred_element_type=F32)


def _split(x):
    hi = _mx(x)
    lo = _mx(x - hi.astype(F32))
    return hi, lo


def _sigmoid(x):
    return 1.0 / (1.0 + jnp.exp(-x))


def _silu(x):
    return x * _sigmoid(x)


def _norm_mod(x, g, shift, scale):
    y = x * lax.rsqrt(jnp.mean(x * x, axis=-1, keepdims=True) + EPS) * g
    return y * (1.0 + scale) + shift


def _cparams(sem):
    return pltpu.CompilerParams(dimension_semantics=sem, vmem_limit_bytes=VMEM_LIMIT)


def _const_spec(shape):
    nd = len(shape)
    return pl.BlockSpec(shape, lambda *_: (0,) * nd, pipeline_mode=pl.Buffered(1))


def _mod_kernel(c_ref, w_ref, b_ref, o_ref):
    s = _silu(c_ref[...])
    o_ref[0] = jnp.dot(s, w_ref[0], preferred_element_type=F32,
                       precision=lax.Precision.HIGHEST) + b_ref[0]


def _modulation(cond, mod_w, mod_b):
    depth, d, n = mod_w.shape
    rows = cond.shape[0]
    tn = 1536
    return pl.pallas_call(
        _mod_kernel,
        out_shape=jax.ShapeDtypeStruct((depth, rows, n), F32),
        grid=(depth, n // tn),
        in_specs=[pl.BlockSpec((rows, d), lambda l, j: (0, 0)),
                  pl.BlockSpec((1, d, tn), lambda l, j: (l, 0, j)),
                  pl.BlockSpec((1, 1, tn), lambda l, j: (l, 0, j))],
        out_specs=pl.BlockSpec((1, rows, tn), lambda l, j: (l, 0, j)),
        compiler_params=_cparams(("arbitrary", "arbitrary")),
    )(cond, mod_w, mod_b.reshape(depth, 1, n))


def _head_rms(u, bd, gain):
    hi, lo = _split(u * u)
    ms = jnp.dot(hi, bd, preferred_element_type=F32) + jnp.dot(lo, bd, preferred_element_type=F32)
    return u * lax.rsqrt(ms + EPS) * gain


def _rope_half_swap(u):
    lane = lax.broadcasted_iota(jnp.int32, u.shape, 1)
    first = (lane % HEAD_DIM) < (HEAD_DIM // 2)
    return jnp.where(first, pltpu.roll(u, LANES - HEAD_DIM // 2, 1), pltpu.roll(u, HEAD_DIM // 2, 1))


def _dup_halves(u):
    lane = lax.broadcasted_iota(jnp.int32, u.shape, 1)
    low = lane < HEAD_DIM
    swapped = pltpu.roll(u, HEAD_DIM, 1)
    return jnp.concatenate([jnp.where(low, u, swapped), jnp.where(low, swapped, u)], axis=1)


def _proj_even_kernel(x_ref, mod_ref, ng_ref, w_ref, gq_ref, gk_ref, bdq_ref, bdk_ref, cs_ref, sn_ref,
                      q_ref, k_ref, v_ref, hg_ref, *, rope):
    h = _mx(_norm_mod(x_ref[0], ng_ref[...], mod_ref[0:1, :], mod_ref[1:2, :]))
    qkv = jnp.dot(h, w_ref[:, :HG_OFF], preferred_element_type=F32)
    cut = HG_OFF + 2 * HGRN_WIDTH
    hg_ref[0, :, :cut - HG_OFF] = jnp.dot(h, w_ref[:, HG_OFF:cut], preferred_element_type=F32)
    q = _head_rms(qkv[:, :A_Q], bdq_ref[...], gq_ref[...])
    k = _head_rms(qkv[:, A_Q:A_Q + A_KV], bdk_ref[...], gk_ref[...])
    hg_ref[0, :, cut - HG_OFF:] = jnp.dot(h, w_ref[:, cut:], preferred_element_type=F32)
    v = qkv[:, A_Q + A_KV:HG_OFF]
    if rope:
        cs, sn = cs_ref[...], sn_ref[...]
        q = jnp.concatenate(
            [q[:, c * LANES:(c + 1) * LANES] * cs + _rope_half_swap(q[:, c * LANES:(c + 1) * LANES]) * sn
             for c in range(A_Q // LANES)], axis=1)
        k = k * cs + _rope_half_swap(k) * sn
    q_ref[0] = (q * (HEAD_DIM ** -0.5 * LOG2E)).astype(q_ref.dtype)
    k_ref[0] = _dup_halves(k).astype(k_ref.dtype)
    v_ref[0] = _dup_halves(v).astype(v_ref.dtype)


def _proj_even(x, mod4, layer, row_of_batch, ng, w, gq, gk, bdq, bdk, cs, sn, *, rope, tm):
    b, l, d = x.shape
    nblk = l // tm
    kern = functools.partial(_proj_even_kernel, rope=rope)
    tab_spec = pl.BlockSpec((tm, LANES), lambda i, j: (j, 0))
    return pl.pallas_call(
        kern,
        out_shape=(jax.ShapeDtypeStruct((b, l, A_Q), MXU_DTYPE),
                   jax.ShapeDtypeStruct((b, l, 2 * A_KV), MXU_DTYPE),
                   jax.ShapeDtypeStruct((b, l, 2 * A_KV), MXU_DTYPE),
                   jax.ShapeDtypeStruct((b, l, 5 * HGRN_WIDTH), F32)),
        grid=(b, nblk),
        in_specs=[pl.BlockSpec((1, tm, d), lambda i, j: (i, j, 0)),
                  pl.BlockSpec((None, None, 6, d), lambda i, j: (layer, row_of_batch(i), 0, 0)),
                  _const_spec((1, d)),
                  _const_spec((d, EVEN_IN)),
                  _const_spec((1, A_Q)), _const_spec((1, A_KV)),
                  _const_spec((A_Q, A_Q)), _const_spec((A_KV, A_KV)),
                  tab_spec, tab_spec],
        out_specs=(pl.BlockSpec((1, tm, A_Q), lambda i, j: (i, j, 0)),
                   pl.BlockSpec((1, tm, 2 * A_KV), lambda i, j: (i, j, 0)),
                   pl.BlockSpec((1, tm, 2 * A_KV), lambda i, j: (i, j, 0)),
                   pl.BlockSpec((1, tm, 5 * HGRN_WIDTH), lambda i, j: (i, j, 0))),
        compiler_params=_cparams(("parallel", "parallel")),
    )(x, mod4, ng, w, gq, gk, bdq, bdk, cs, sn)


def _attn_kernel(*refs, window, tq, seq):
    if window:
        (q_ref, kp_ref, kc_ref, kn_ref, vp_ref, vc_ref, vn_ref, kx_ref, vx_ref, sink_ref, o_ref) = refs
    else:
        (q_ref, kx_ref, vx_ref, sink_ref, o_ref) = refs
    i = pl.program_id(1)
    n_ctx = kx_ref.shape[1]
    sub = WINDOW
    span = 3 * sub
    lane = lax.broadcasted_iota(jnp.int32, (2 * sub, LANES), 1)
    low = lane < HEAD_DIM
    if window:
        rr = lax.broadcasted_iota(jnp.int32, (sub, sub), 0)
        cc = lax.broadcasted_iota(jnp.int32, (sub, sub), 1)
    ones = jnp.ones((n_ctx + (span if window else 0), LANES), MXU_DTYPE)

    def scores(sb, g):
        rows = slice(sb * sub, (sb + 1) * sub)
        cols = slice(g * LANES, (g + 1) * LANES)
        if window:
            kfull = jnp.concatenate([kp_ref[0, :, cols], kc_ref[0, :, cols], kn_ref[0, :, cols]], axis=0)
            kk = jnp.concatenate([kfull[sb * sub:sb * sub + span], kx_ref[0, :, cols]], axis=0)
        else:
            kk = kx_ref[0, :, cols]
        q2 = jnp.concatenate([q_ref[0, rows, (2 * g) * LANES:(2 * g + 1) * LANES],
                              q_ref[0, rows, (2 * g + 1) * LANES:(2 * g + 2) * LANES]], axis=0)
        zero = jnp.zeros_like(q2)
        q4 = jnp.concatenate([jnp.where(low, q2, zero), jnp.where(low, zero, q2)], axis=0)
        return _dot_nt(q4, kk)

    def finish(sb, g, s):
        rows = slice(sb * sub, (sb + 1) * sub)
        cols = slice(g * LANES, (g + 1) * LANES)
        tiles = [s[:, t * LANES:(t + 1) * LANES] for t in range(s.shape[1] // LANES)]
        if window:
            base = i * tq + sb * sub
            ok_prev = (cc >= rr) & (base - sub + cc >= 0)
            ok_next = (cc <= rr) & (base + sub + cc < seq)
            tiles[0] = tiles[0] + jnp.concatenate([jnp.where(ok_prev, 0.0, NEG)] * 4, axis=0)
            tiles[2] = tiles[2] + jnp.concatenate([jnp.where(ok_next, 0.0, NEG)] * 4, axis=0)
            s = jnp.concatenate(tiles, axis=1)
            vfull = jnp.concatenate([vp_ref[0, :, cols], vc_ref[0, :, cols], vn_ref[0, :, cols]], axis=0)
            vv = jnp.concatenate([vfull[sb * sub:sb * sub + span], vx_ref[0, :, cols]], axis=0)
        else:
            vv = vx_ref[0, :, cols]
        tmax = functools.reduce(jnp.maximum, tiles)
        sink = sink_ref[g]
        m = jnp.maximum(jnp.max(tmax, axis=-1, keepdims=True), sink)
        p = jnp.exp2(s - m)
        pv = _dot(p, jnp.concatenate([vv, ones], axis=1))
        o = pv[:, :LANES] * (1.0 / (pv[:, LANES:] + jnp.exp2(sink - m)))
        o2 = jnp.where(low, o[:2 * sub], o[2 * sub:])
        o_ref[0, rows, (2 * g) * LANES:(2 * g + 1) * LANES] = o2[:sub].astype(o_ref.dtype)
        o_ref[0, rows, (2 * g + 1) * LANES:(2 * g + 2) * LANES] = o2[sub:].astype(o_ref.dtype)

    units = [(sb, g) for sb in range(tq // sub) for g in range(ATTN_KV_HEADS)]
    ahead = 1
    pending = [scores(*u) for u in units[:ahead]]
    for idx, unit in enumerate(units):
        if idx + ahead < len(units):
            pending.append(scores(*units[idx + ahead]))
        finish(*unit, pending.pop(0))


def _sink_rows(sink, tq):
    s = sink.astype(F32).reshape(ATTN_KV_HEADS, 4) * LOG2E
    order = jnp.array([0, 2, 1, 3])
    return jnp.repeat(s[:, order], tq, axis=1)[:, :, None]


def _attention(q, k, v, kx, vx, sink, *, window, tq):
    b, l, _ = q.shape
    n_ctx = kx.shape[1]
    nq = l // tq
    kern = functools.partial(_attn_kernel, window=window, tq=tq, seq=l)
    qspec = pl.BlockSpec((1, tq, A_Q), lambda i, j: (i, j, 0))
    xspec = pl.BlockSpec((1, n_ctx, 2 * A_KV), lambda i, j: (i, 0, 0))
    sspec = _const_spec((ATTN_KV_HEADS, 4 * WINDOW, 1))
    if window:
        r = tq // WINDOW
        prev = pl.BlockSpec((1, WINDOW, 2 * A_KV), lambda i, j: (i, jnp.maximum(j * r - 1, 0), 0))
        cur = pl.BlockSpec((1, tq, 2 * A_KV), lambda i, j: (i, j, 0))
        nxt = pl.BlockSpec((1, WINDOW, 2 * A_KV), lambda i, j: (i, jnp.minimum((j + 1) * r, nq * r - 1), 0))
        in_specs = [qspec, prev, cur, nxt, prev, cur, nxt, xspec, xspec, sspec]
        args = (q, k, k, k, v, v, v, kx, vx, _sink_rows(sink, WINDOW))
    else:
        in_specs = [qspec, xspec, xspec, sspec]
        args = (q, kx, vx, _sink_rows(sink, WINDOW))
    return pl.pallas_call(
        kern,
        out_shape=jax.ShapeDtypeStruct((b, l, A_Q), MXU_DTYPE),
        grid=(b, nq),
        in_specs=in_specs,
        out_specs=pl.BlockSpec((1, tq, A_Q), lambda i, j: (i, j, 0)),
        compiler_params=_cparams(("parallel", "parallel")),
    )(*args)


class _HgrnDirection:
    def __init__(self, q_ref, f_ref, i_ref, st_ref, o_ref, *, rev, tb):
        self.q_ref, self.f_ref, self.i_ref, self.st_ref, self.o_ref = q_ref, f_ref, i_ref, st_ref, o_ref
        self.rev, self.tb, self.nc = rev, tb, tb // GLA_CHUNK

    def gates(self, lb):
        tb, nc, c_len, width = self.tb, self.nc, GLA_CHUNK, HGRN_WIDTH
        self.qs3 = _silu(self.q_ref[0]).reshape(nc, c_len, width)
        f = lb + (1.0 - lb) * _sigmoid(self.f_ref[0])
        self.kf3 = (1.0 - f).reshape(nc, c_len, width)
        lf = jnp.log(f)
        self.lf3 = lf.reshape(nc, c_len, width)
        self.iv = _mx(self.i_ref[0])
        r = lax.broadcasted_iota(jnp.int32, (tb, tb), 0)
        c = lax.broadcasted_iota(jnp.int32, (tb, tb), 1)
        same = (r // c_len) == (c // c_len)
        lower = same & (c <= r)
        self.keep = (same & (c >= r)) if self.rev else lower
        tri = _mx(jnp.where(lower, 1.0, 0.0))
        hi, lo = _split(lf)
        self.cum = (jnp.dot(tri, hi, preferred_element_type=F32)
                    + jnp.dot(tri, lo, preferred_element_type=F32)).reshape(nc, c_len, width)

    def first_matmuls(self):
        tb, nc, c_len, width = self.tb, self.nc, GLA_CHUNK, HGRN_WIDTH
        cum = self.cum
        if self.rev:
            w3 = cum[:, c_len - 1:c_len, :] - cum + self.lf3
            edge = w3[:, 0:1, :]
        else:
            w3 = cum
            edge = cum[:, c_len - 1:c_len, :]
        mid = w3[:, c_len // 2:c_len // 2 + 1, :]
        qa3 = self.qs3 * jnp.exp(w3 - mid)
        kb3 = self.kf3 * jnp.exp(mid - w3)
        qa = _mx(qa3.reshape(tb, width))
        kb = _mx(kb3.reshape(tb, width))
        self.qe = _mx((qa3 * jnp.exp(mid)).reshape(tb, width))
        ke = _mx((kb3 * jnp.exp(edge - mid)).reshape(tb, width))
        self.dec = jnp.exp(edge)
        iv = self.iv
        self.scores, self.kv = [], []
        for h in range(HGRN_HEADS):
            hs = slice(h * HGRN_DK, (h + 1) * HGRN_DK)
            self.scores.append(_dot_nt(qa[:, hs], kb[:, hs]))
            self.kv.append([_dot_tn(iv[ci * c_len:(ci + 1) * c_len, hs], ke[ci * c_len:(ci + 1) * c_len, hs])
                            for ci in range(nc)])

    def second_matmuls(self):
        nc, c_len = self.nc, GLA_CHUNK
        order = range(nc - 1, -1, -1) if self.rev else range(nc)
        for h in range(HGRN_HEADS):
            hs = slice(h * HGRN_DK, (h + 1) * HGRN_DK)
            o_intra = _dot(jnp.where(self.keep, self.scores[h], 0.0), self.iv[:, hs])
            st = self.st_ref[h]
            for ci in order:
                rows = slice(ci * c_len, (ci + 1) * c_len)
                self.o_ref[0, rows, hs] = o_intra[rows] + _dot_nt(self.qe[rows, hs], st)
                st = st * self.dec[ci, :, hs] + self.kv[h][ci]
            self.st_ref[h] = st


def _hgrn_kernel(qf_ref, ff_ref, if_ref, qb_ref, fb_ref, ib_ref, lb_ref, s0f_ref, s0b_ref,
                 of_ref, ob_ref, sff_ref, sfb_ref, st_ref, *, tb):
    j = pl.program_id(1)

    @pl.when(j == 0)
    def _():
        st_ref[0] = s0f_ref[0]
        st_ref[1] = s0b_ref[0]

    lb = lb_ref[...]
    fw = _HgrnDirection(qf_ref, ff_ref, if_ref, st_ref.at[0], of_ref, rev=False, tb=tb)
    bw = _HgrnDirection(qb_ref, fb_ref, ib_ref, st_ref.at[1], ob_ref, rev=True, tb=tb)
    fw.gates(lb)
    bw.gates(lb)
    fw.first_matmuls()
    bw.first_matmuls()
    fw.second_matmuls()
    bw.second_matmuls()

    @pl.when(j == pl.num_programs(1) - 1)
    def _():
        sff_ref[0] = st_ref[0]
        sfb_ref[0] = st_ref[1]


def _hgrn_scan(hg, lb, s0_fw, s0_bw, *, tb):
    b, l, _ = hg.shape
    nb = l // tb
    fcol = lambda cidx: pl.BlockSpec((1, tb, HGRN_WIDTH), lambda i, j: (i, j, cidx))
    bcol = lambda cidx: pl.BlockSpec((1, tb, HGRN_WIDTH), lambda i, j: (i, nb - 1 - j, cidx))
    st_shape = (HGRN_HEADS, HGRN_DK, HGRN_DK)
    st_spec = pl.BlockSpec((1,) + st_shape, lambda i, j: (i, 0, 0, 0))
    o_shape = jax.ShapeDtypeStruct((b, l, HGRN_WIDTH), F32)
    s_shape = jax.ShapeDtypeStruct((b,) + st_shape, F32)
    return pl.pallas_call(
        functools.partial(_hgrn_kernel, tb=tb),
        out_shape=(o_shape, o_shape, s_shape, s_shape),
        grid=(b, nb),
        in_specs=[fcol(0), fcol(1), fcol(3), bcol(0), bcol(2), bcol(3),
                  _const_spec((1, HGRN_WIDTH)), st_spec, st_spec],
        out_specs=(fcol(0), bcol(0), st_spec, st_spec),
        scratch_shapes=[pltpu.VMEM((2,) + st_shape, F32)],
        compiler_params=_cparams(("parallel", "arbitrary")),
    )(hg, hg, hg, hg, hg, hg, lb, s0_fw, s0_bw)


def _swiglu_residual(x1, mod_ref, ng2_ref, wg_ref, wu_ref, wd_ref):
    h2 = _mx(_norm_mod(x1, ng2_ref[...], mod_ref[3:4, :], mod_ref[4:5, :]))
    acc = None
    for ci in range(wg_ref.shape[0]):
        gate = jnp.dot(h2, wg_ref[ci], preferred_element_type=F32)
        up = jnp.dot(h2, wu_ref[ci], preferred_element_type=F32)
        part = _dot(_silu(gate) * up, wd_ref[ci])
        acc = part if acc is None else acc + part
    return x1 + mod_ref[5:6, :] * acc


def _out_kernel(*refs, n_mix, gated):
    mixes = [m_ref[0] for m_ref in refs[:n_mix]]
    rest = refs[n_mix:]
    if gated:
        of_ref, ob_ref, g_ref, og_ref = rest[:4]
        rest = rest[4:]
        o = of_ref[0] + ob_ref[0]
        og = og_ref[...]
        parts = []
        for h in range(HGRN_HEADS):
            oh = o[:, h * HGRN_DK:(h + 1) * HGRN_DK]
            parts.append(oh * lax.rsqrt(jnp.mean(oh * oh, axis=-1, keepdims=True) + EPS) * og)
        mixes.append(_mx(jnp.concatenate(parts, axis=1) * _silu(g_ref[0])))
    x_ref, mod_ref, ng2_ref, wo_ref, wg_ref, wu_ref, wd_ref, o_ref = rest
    y, off = None, 0
    for m in mixes:
        w = m.shape[-1]
        part = jnp.dot(m, wo_ref[off:off + w, :], preferred_element_type=F32)
        y = part if y is None else y + part
        off += w
    x1 = x_ref[0] + mod_ref[2:3, :] * y
    o_ref[0] = _swiglu_residual(x1, mod_ref, ng2_ref, wg_ref, wu_ref, wd_ref)


def _out_layer(mixes, x, mod4, layer, row_of_batch, ng2, wo, wg, wu, wd, *, tm, gated=None):
    b, l, d = x.shape
    nch = wg.shape[0]
    row = lambda w: pl.BlockSpec((1, tm, w), lambda i, j: (i, j, 0))
    in_specs = [row(m.shape[-1]) for m in mixes]
    args = list(mixes)
    if gated is not None:
        in_specs += [row(HGRN_WIDTH), row(HGRN_WIDTH),
                     pl.BlockSpec((1, tm, HGRN_WIDTH), lambda i, j: (i, j, 4)), _const_spec((1, HGRN_DK))]
        args += list(gated)
    in_specs += [row(d),
                 pl.BlockSpec((None, None, 6, d), lambda i, j: (layer, row_of_batch(i), 0, 0)),
                 _const_spec((1, d)), _const_spec(wo.shape),
                 _const_spec((nch, d, FFN_CHUNK)), _const_spec((nch, d, FFN_CHUNK)),
                 _const_spec((nch, FFN_CHUNK, d))]
    return pl.pallas_call(
        functools.partial(_out_kernel, n_mix=len(mixes), gated=gated is not None),
        out_shape=jax.ShapeDtypeStruct((b, l, d), F32),
        grid=(b, l // tm),
        in_specs=in_specs,
        out_specs=row(d),
        compiler_params=_cparams(("parallel", "parallel")),
    )(*args, x, mod4, ng2, wo, wg, wu, wd)


def _rope_ret(u, cs, sn):
    parts = []
    for h in range(RET_HEADS):
        x1 = u[:, h * RET_DK:h * RET_DK + LANES]
        x2 = u[:, h * RET_DK + LANES:(h + 1) * RET_DK]
        parts += [x1 * cs - x2 * sn, x2 * cs + x1 * sn]
    return jnp.concatenate(parts, axis=1)


def _ret_log_gamma(h, rev):
    hh = RET_HEADS - 1 - h if rev else h
    return math.log(1.0 - 2.0 ** (-5.0 - hh))


def _ret_weights(n, lg, rev):
    pos = lax.broadcasted_iota(jnp.int32, (n, 1), 0).astype(F32)
    if rev:
        return jnp.exp(lg * pos), jnp.exp(lg * (n - pos))
    return jnp.exp(lg * (n - 1.0 - pos)), jnp.exp(lg * (pos + 1.0))


def _ret_init_state(kc_ref, vc_ref, s_ref, rev):
    lc = kc_ref.shape[1]
    for h in range(RET_HEADS):
        kw, _ = _ret_weights(lc, _ret_log_gamma(h, rev), rev)
        kc = kc_ref[0, :, h * RET_DK:(h + 1) * RET_DK].astype(F32) * kw
        s_ref[h] = _dot_tn(kc, vc_ref[0, :, h * RET_DV:(h + 1) * RET_DV])


def _proj_odd_kernel(x_ref, mod_ref, ng_ref, w_ref, cs_ref, sn_ref, kc_ref, vc_ref,
                     q_ref, k_ref, v_ref, g_ref, ob_ref, s_ref, *, chunk):
    @pl.when(pl.program_id(1) == 0)
    def _():
        _ret_init_state(kc_ref, vc_ref, s_ref, True)

    h = _mx(_norm_mod(x_ref[0], ng_ref[...], mod_ref[0:1, :], mod_ref[1:2, :]))
    cs, sn = cs_ref[...], sn_ref[...]
    g = jnp.dot(h, w_ref[:, 2 * ODD_Q + ODD_V:], preferred_element_type=F32)
    g_ref[0] = _silu(g).astype(g_ref.dtype)
    q = _mx(_rope_ret(jnp.dot(h, w_ref[:, :ODD_Q], preferred_element_type=F32), cs, sn))
    q_ref[0] = q
    k = _rope_ret(jnp.dot(h, w_ref[:, ODD_Q:2 * ODD_Q], preferred_element_type=F32), cs, sn) * RET_DK ** -0.5
    k_ref[0] = k.astype(k_ref.dtype)
    v = _mx(jnp.dot(h, w_ref[:, 2 * ODD_Q:2 * ODD_Q + ODD_V], preferred_element_type=F32))
    v_ref[0] = v
    for ci in range(x_ref.shape[1] // chunk - 1, -1, -1):
        rows = slice(ci * chunk, (ci + 1) * chunk)
        for hd in range(RET_HEADS):
            lg = _ret_log_gamma(hd, True)
            kw, qw = _ret_weights(chunk, lg, True)
            ks, vs = slice(hd * RET_DK, (hd + 1) * RET_DK), slice(hd * RET_DV, (hd + 1) * RET_DV)
            s = s_ref[hd]
            ob_ref[0, rows, vs] = (_dot(q[rows, ks], s) * qw).astype(ob_ref.dtype)
            s_ref[hd] = s * math.exp(lg * chunk) + _dot_tn(k[rows, ks] * kw, v[rows, vs])


def _proj_odd(x, mod4, layer, ng, w, cs, sn, kc, vc, *, tm, chunk):
    b, l, d = x.shape
    lc = kc.shape[1]
    nb = l // tm
    row = lambda wd: pl.BlockSpec((1, tm, wd), lambda i, j: (i, nb - 1 - j, 0))
    tab = pl.BlockSpec((tm, LANES), lambda i, j: (nb - 1 - j, 0))
    ctx = lambda wd: pl.BlockSpec((1, lc, wd), lambda i, j: (i, 0, 0))
    wide = jax.ShapeDtypeStruct((b, l, ODD_V), MXU_DTYPE)
    narrow = jax.ShapeDtypeStruct((b, l, ODD_Q), MXU_DTYPE)
    return pl.pallas_call(
        functools.partial(_proj_odd_kernel, chunk=chunk),
        out_shape=(narrow, narrow, wide, wide, wide),
        grid=(b, nb),
        in_specs=[row(d),
                  pl.BlockSpec((None, None, 6, d), lambda i, j: (layer, i, 0, 0)),
                  _const_spec((1, d)), _const_spec((d, ODD_IN)), tab, tab, ctx(ODD_Q), ctx(ODD_V)],
        out_specs=(row(ODD_Q), row(ODD_Q), row(ODD_V), row(ODD_V), row(ODD_V)),
        scratch_shapes=[pltpu.VMEM((RET_HEADS, RET_DK, RET_DV), F32)],
        compiler_params=_cparams(("parallel", "arbitrary")),
    )(x, mod4, ng, w, cs, sn, kc, vc)


def _proj_ctx_kv_kernel(x_ref, mod_ref, ng_ref, w_ref, k_ref, v_ref):
    h = _mx(_norm_mod(x_ref[0], ng_ref[...], mod_ref[0:1, :], mod_ref[1:2, :]))
    k = jnp.dot(h, w_ref[:, :ODD_Q], preferred_element_type=F32)
    k_ref[0] = (k * RET_DK ** -0.5).astype(k_ref.dtype)
    v_ref[0] = jnp.dot(h, w_ref[:, ODD_Q:], preferred_element_type=F32).astype(v_ref.dtype)


def _proj_ctx_kv(ctx, mod4, layer, ctx_row, ng, w_kv):
    b, lc, d = ctx.shape
    return pl.pallas_call(
        _proj_ctx_kv_kernel,
        out_shape=(jax.ShapeDtypeStruct((b, lc, ODD_Q), MXU_DTYPE),
                   jax.ShapeDtypeStruct((b, lc, ODD_V), MXU_DTYPE)),
        grid=(b,),
        in_specs=[pl.BlockSpec((1, lc, d), lambda i: (i, 0, 0)),
                  pl.BlockSpec((None, None, 6, d), lambda i: (layer, ctx_row, 0, 0)),
                  _const_spec((1, d)), _const_spec((d, ODD_Q + ODD_V))],
        out_specs=(pl.BlockSpec((1, lc, ODD_Q), lambda i: (i, 0, 0)),
                   pl.BlockSpec((1, lc, ODD_V), lambda i: (i, 0, 0))),
        compiler_params=_cparams(("parallel",)),
    )(ctx, mod4, ng, w_kv)


def _ret_kernel(q_ref, k_ref, v_ref, kc_ref, vc_ref, dm_ref, ob_ref, sg_ref, o_ref, s_ref, *, tb):
    @pl.when(pl.program_id(1) == 0)
    def _():
        _ret_init_state(kc_ref, vc_ref, s_ref, False)

    heads = []
    for h in range(RET_HEADS):
        lg = _ret_log_gamma(h, False)
        kw, qw = _ret_weights(tb, lg, False)
        q = q_ref[0, :, h * RET_DK:(h + 1) * RET_DK]
        k = k_ref[0, :, h * RET_DK:(h + 1) * RET_DK]
        v = v_ref[0, :, h * RET_DV:(h + 1) * RET_DV]
        s = s_ref[h]
        o = _dot(q, s) * qw
        sc = _dot_nt(q, k)
        s_ref[h] = s * math.exp(lg * tb) + _dot_tn(k.astype(F32) * kw, v)
        heads.append((o, sc, v))
    for h, (o, sc, v) in enumerate(heads):
        vs = slice(h * RET_DV, (h + 1) * RET_DV)
        o = o + _dot(sc * dm_ref[h], v) + ob_ref[0, :, vs].astype(F32)
        o = o * lax.rsqrt(jnp.mean(o * o, axis=-1, keepdims=True) + EPS)
        o_ref[0, :, vs] = (o * sg_ref[0, :, vs].astype(F32)).astype(o_ref.dtype)


def _ret_decay_matrix(tb):
    rel = np.arange(tb)[:, None] - np.arange(tb)[None, :]
    out = np.zeros((RET_HEADS, tb, tb), np.float64)
    for h in range(RET_HEADS):
        fw = np.exp(_ret_log_gamma(h, False) * np.maximum(rel, 0))
        bw = np.exp(_ret_log_gamma(h, True) * np.maximum(-rel, 0))
        out[h] = np.where(rel > 0, fw, np.where(rel < 0, bw, 2.0))
    return jnp.asarray(out, F32)


def _ret_scan(q, k, v, kc, vc, o_bw, sg, *, tb):
    b, l, _ = q.shape
    lc = kc.shape[1]
    row = lambda w: pl.BlockSpec((1, tb, w), lambda i, j: (i, j, 0))
    ctx = lambda w: pl.BlockSpec((1, lc, w), lambda i, j: (i, 0, 0))
    return pl.pallas_call(
        functools.partial(_ret_kernel, tb=tb),
        out_shape=jax.ShapeDtypeStruct((b, l, ODD_V), MXU_DTYPE),
        grid=(b, l // tb),
        in_specs=[row(ODD_Q), row(ODD_Q), row(ODD_V), ctx(ODD_Q), ctx(ODD_V),
                  _const_spec((RET_HEADS, tb, tb)), row(ODD_V), row(ODD_V)],
        out_specs=row(ODD_V),
        scratch_shapes=[pltpu.VMEM((RET_HEADS, RET_DK, RET_DV), F32)],
        compiler_params=_cparams(("parallel", "arbitrary")),
    )(q, k, v, kc, vc, _ret_decay_matrix(tb), o_bw, sg)


def _axial_tables(n_tok):
    t = np.arange(n_tok)
    n_freq = HEAD_DIM // 4
    inv = ROPE_BASE ** (-np.arange(n_freq, dtype=np.float64) / n_freq)
    ang = np.concatenate([(t // GRID_W)[:, None] * inv, (t % GRID_W)[:, None] * inv], axis=-1)
    cos, sin = np.cos(ang), np.sin(ang)
    cs = np.tile(np.concatenate([cos, cos], axis=-1), (1, LANES // HEAD_DIM))
    sn = np.tile(np.concatenate([-sin, sin], axis=-1), (1, LANES // HEAD_DIM))
    return jnp.asarray(cs, F32), jnp.asarray(sn, F32)


def _retention_tables(n_tok):
    theta = 1.0 / (RET_BASE ** np.linspace(0.0, 1.0, RET_DK // 2))
    ang = np.arange(n_tok, dtype=np.float64)[:, None] * theta
    return jnp.asarray(np.cos(ang), F32), jnp.asarray(np.sin(ang), F32)


def _block_diag_mean(width, head):
    idx = np.arange(width) // head
    return jnp.asarray((idx[:, None] == idx[None, :]) / head, MXU_DTYPE)


def _ffn_weights(w_in, w_out):
    d = w_in.shape[0]
    nch = FFN_HIDDEN // FFN_CHUNK
    wg = _mx(w_in[:, :FFN_HIDDEN]).reshape(d, nch, FFN_CHUNK).transpose(1, 0, 2)
    wu = _mx(w_in[:, FFN_HIDDEN:]).reshape(d, nch, FFN_CHUNK).transpose(1, 0, 2)
    wd = _mx(w_out).reshape(nch, FFN_CHUNK, d)
    return wg, wu, wd


def _pick(n, pref):
    while n % pref:
        pref //= 2
    return pref


def kernel(x, c, ctx, c_ctx, mod_w, mod_b, norm_g, ffn_w_in, ffn_w_out, even_w_in, even_w_out,
           attn_qk_norm_g, attn_sink, hgrn_out_norm_g, hgrn_lb, odd_w_in, odd_w_out):
    n_b, n_tok, d = x.shape
    n_ctx = ctx.shape[1]
    depth = mod_w.shape[0]
    assert depth == 2 and d == D_MODEL

    rows = -(-(n_b + 1) // 8) * 8
    cond = jnp.zeros((rows, d), F32).at[:n_b].set(c).at[n_b].set(c_ctx)
    mod4 = _modulation(cond, mod_w, mod_b).reshape(depth, rows, 6, d)
    lat_row = lambda i: i
    ctx_row = lambda i: n_b

    ng1, ng2 = norm_g[0, 0][None, :], norm_g[0, 1][None, :]
    w_in = _mx(even_w_in[0])
    gq = jnp.tile(attn_qk_norm_g[0, 0], ATTN_HEADS)[None, :]
    gk = jnp.tile(attn_qk_norm_g[0, 1], ATTN_KV_HEADS)[None, :]
    bdq, bdk = _block_diag_mean(A_Q, HEAD_DIM), _block_diag_mean(A_KV, HEAD_DIM)
    cs_a, sn_a = _axial_tables(n_tok)
    lower = jnp.cumsum(jax.nn.softmax(hgrn_lb.astype(F32), axis=0), axis=0)[0][None, :]
    og = hgrn_out_norm_g[0][None, :]
    wo = _mx(even_w_out[0])
    wg, wu, wd = _ffn_weights(ffn_w_in[0], ffn_w_out[0])

    tm_c = _pick(n_ctx, 256)
    tm_l = _pick(n_tok, 512)
    ones_c = jnp.ones((n_ctx, LANES), F32)
    qc, kc, vc, hgc = _proj_even(ctx, mod4, 0, ctx_row, ng1, w_in, gq, gk, bdq, bdk, ones_c, ones_c,
                                 rope=False, tm=tm_c)
    ql, kl, vl, hgl = _proj_even(x, mod4, 0, lat_row, ng1, w_in, gq, gk, bdq, bdk, cs_a, sn_a,
                                 rope=True, tm=tm_l)

    a_ctx = _attention(qc, None, None, kc, vc, attn_sink[0], window=False, tq=WINDOW)
    a_lat = _attention(ql, kl, vl, kc, vc, attn_sink[0], window=True, tq=_pick(n_tok, 4 * WINDOW))

    zeros = jnp.zeros((n_b, HGRN_HEADS, HGRN_DK, HGRN_DK), F32)
    tb_c, tb_l = _pick(n_ctx, 256), _pick(n_tok, 256)
    o_fw_c, o_bw_c, s_fw, s_bw = _hgrn_scan(hgc, lower, zeros, zeros, tb=tb_c)
    o_fw, o_bw, _, _ = _hgrn_scan(hgl, lower, s_fw, s_bw, tb=tb_l)

    ctx1 = _out_layer([a_ctx], ctx, mod4, 0, ctx_row, ng2, wo, wg, wu, wd, tm=tm_c,
                      gated=(o_fw_c, o_bw_c, hgc, og))
    x1 = _out_layer([a_lat], x, mod4, 0, lat_row, ng2, wo, wg, wu, wd, tm=tm_l,
                    gated=(o_fw, o_bw, hgl, og))

    ng1, ng2 = norm_g[1, 0][None, :], norm_g[1, 1][None, :]
    w_in = _mx(odd_w_in[0])
    cs_r, sn_r = _retention_tables(n_tok)
    wo = _mx(odd_w_out[0])
    wg, wu, wd = _ffn_weights(ffn_w_in[1], ffn_w_out[1])

    kc, vc = _proj_ctx_kv(ctx1, mod4, 1, n_b, ng1, w_in[:, ODD_Q:2 * ODD_Q + ODD_V])
    tb_r = _pick(n_tok, 256)
    q, k, v, sg, o_bw = _proj_odd(x1, mod4, 1, ng1, w_in, cs_r, sn_r, kc, vc, tm=_pick(n_tok, 512), chunk=tb_r)
    mix = _ret_scan(q, k, v, kc, vc, o_bw, sg, tb=tb_r)
    return _out_layer([mix], x1, mod4, 1, lat_row, ng2, wo, wg, wu, wd, tm=_pick(n_tok, 512))
```

```python
import functools
import math

import numpy as np
import jax
import jax.numpy as jnp
from jax import lax
from jax.experimental import pallas as pl
from jax.experimental.pallas import tpu as pltpu

F32 = jnp.float32
MXU_DTYPE = jnp.bfloat16

D_MODEL = 1024
GRID_W = 64
HEAD_DIM = 64
ATTN_HEADS = 8
ATTN_KV_HEADS = 2
WINDOW = 128
ROPE_BASE = 10000.0
HGRN_HEADS = 4
HGRN_DK = 128
HGRN_WIDTH = HGRN_HEADS * HGRN_DK
GLA_CHUNK = 64
RET_HEADS = 4
RET_DK = 256
RET_DV = 512
RET_BASE = 10000.0
FFN_HIDDEN = 2816
A_Q = ATTN_HEADS * HEAD_DIM
A_KV = ATTN_KV_HEADS * HEAD_DIM
EVEN_IN = A_Q + 2 * A_KV + 5 * HGRN_WIDTH
HG_OFF = A_Q + 2 * A_KV
ODD_Q = RET_HEADS * RET_DK
ODD_V = RET_HEADS * RET_DV
ODD_IN = 2 * ODD_Q + 2 * ODD_V
EPS = 1e-6
NEG = -1e30
LOG2E = 1.4426950408889634

LANES = 128
FFN_CHUNK = 256
VMEM_LIMIT = 56 * 1024 * 1024


def _mx(a):
    return a.astype(MXU_DTYPE)


def _dot(a, b):
    return jnp.dot(_mx(a), _mx(b), preferred_element_type=F32)


def _dot_nt(a, b):
    return lax.dot_general(_mx(a), _mx(b), (((1,), (1,)), ((), ())), preferred_element_type=F32)


def _dot_tn(a, b):
    return lax.dot_general(_mx(a), _mx(b), (((0,), (0,)), ((), ())), preferred_element_type=F32)


def _split(x):
    hi = _mx(x)
    lo = _mx(x - hi.astype(F32))
    return hi, lo


def _sigmoid(x):
    return 1.0 / (1.0 + jnp.exp(-x))


def _silu(x):
    return x * _sigmoid(x)


def _norm_mod(x, g, shift, scale):
    y = x * lax.rsqrt(_mean_rows(x * x) + EPS) * g
    return y * (1.0 + scale) + shift


def _lane_tiles(x):
    return [x[:, t * LANES:(t + 1) * LANES] for t in range(x.shape[1] // LANES)]


def _scale_rows(x, tab):
    return jnp.concatenate([t * tab for t in _lane_tiles(x)], axis=1)


def _mean_rows(x):
    return jnp.sum(functools.reduce(jnp.add, _lane_tiles(x)), axis=-1, keepdims=True) * (1.0 / x.shape[1])


def _cparams(sem):
    return pltpu.CompilerParams(dimension_semantics=sem, vmem_limit_bytes=VMEM_LIMIT)


def _const_spec(shape):
    nd = len(shape)
    return pl.BlockSpec(shape, lambda *_: (0,) * nd, pipeline_mode=pl.Buffered(1))


def _mod_kernel(c_ref, w_ref, b_ref, o_ref):
    s = _silu(c_ref[...])
    o_ref[0] = jnp.dot(s, w_ref[0], preferred_element_type=F32,
                       precision=lax.Precision.HIGHEST) + b_ref[0]


def _modulation(cond, mod_w, mod_b):
    depth, d, n = mod_w.shape
    rows = cond.shape[0]
    tn = 1536
    return pl.pallas_call(
        _mod_kernel,
        out_shape=jax.ShapeDtypeStruct((depth, rows, n), F32),
        grid=(depth, n // tn),
        in_specs=[pl.BlockSpec((rows, d), lambda l, j: (0, 0)),
                  pl.BlockSpec((1, d, tn), lambda l, j: (l, 0, j)),
                  pl.BlockSpec((1, 1, tn), lambda l, j: (l, 0, j))],
        out_specs=pl.BlockSpec((1, rows, tn), lambda l, j: (l, 0, j)),
        compiler_params=_cparams(("arbitrary", "arbitrary")),
    )(cond, mod_w, mod_b.reshape(depth, 1, n))


def _head_rms(u, bd, gain):
    hi, lo = _split(u * u)
    ms = jnp.dot(hi, bd, preferred_element_type=F32) + jnp.dot(lo, bd, preferred_element_type=F32)
    return u * lax.rsqrt(ms + EPS) * gain


def _rope_half_swap(u):
    lane = lax.broadcasted_iota(jnp.int32, u.shape, 1)
    first = (lane % HEAD_DIM) < (HEAD_DIM // 2)
    return jnp.where(first, pltpu.roll(u, LANES - HEAD_DIM // 2, 1), pltpu.roll(u, HEAD_DIM // 2, 1))


def _dup_halves(u):
    lane = lax.broadcasted_iota(jnp.int32, u.shape, 1)
    low = lane < HEAD_DIM
    swapped = pltpu.roll(u, HEAD_DIM, 1)
    return jnp.concatenate([jnp.where(low, u, swapped), jnp.where(low, swapped, u)], axis=1)


def _proj_even_kernel(x_ref, mod_ref, ng_ref, w_ref, gq_ref, gk_ref, bdq_ref, bdk_ref, cs_ref, sn_ref, lb_ref,
                      q_ref, k_ref, v_ref, hq_ref, hl_ref, hi_ref, hg_ref, *, rope):
    h = _mx(_norm_mod(x_ref[0], ng_ref[...], mod_ref[0:1, :], mod_ref[1:2, :]))
    lb = lb_ref[...]
    col = lambda c: w_ref[:, HG_OFF + c * HGRN_WIDTH:HG_OFF + (c + 1) * HGRN_WIDTH]
    log_gate = lambda raw: jnp.log(lb + (1.0 - lb) * _sigmoid(raw))
    hl_ref[0, :, :HGRN_WIDTH] = log_gate(jnp.dot(h, col(1), preferred_element_type=F32))
    hl_ref[0, :, HGRN_WIDTH:] = log_gate(jnp.dot(h, col(2), preferred_element_type=F32))
    qkv = jnp.dot(h, w_ref[:, :HG_OFF], preferred_element_type=F32)
    hg_ref[0] = _silu(jnp.dot(h, col(4), preferred_element_type=F32)).astype(hg_ref.dtype)
    q = _head_rms(qkv[:, :A_Q], bdq_ref[...], gq_ref[...])
    k = _head_rms(qkv[:, A_Q:A_Q + A_KV], bdk_ref[...], gk_ref[...])
    hq_ref[0] = _silu(jnp.dot(h, col(0), preferred_element_type=F32))
    hi_ref[0] = jnp.dot(h, col(3), preferred_element_type=F32).astype(hi_ref.dtype)
    v = qkv[:, A_Q + A_KV:HG_OFF]
    if rope:
        cs, sn = cs_ref[...], sn_ref[...]
        q = jnp.concatenate(
            [q[:, c * LANES:(c + 1) * LANES] * cs + _rope_half_swap(q[:, c * LANES:(c + 1) * LANES]) * sn
             for c in range(A_Q // LANES)], axis=1)
        k = k * cs + _rope_half_swap(k) * sn
    q_ref[0] = (q * (HEAD_DIM ** -0.5 * LOG2E)).astype(q_ref.dtype)
    k_ref[0] = _dup_halves(k).astype(k_ref.dtype)
    v_ref[0] = _dup_halves(v).astype(v_ref.dtype)


def _proj_even(x, mod4, layer, row_of_batch, ng, w, gq, gk, bdq, bdk, cs, sn, lb, *, rope, tm):
    b, l, d = x.shape
    nblk = l // tm
    kern = functools.partial(_proj_even_kernel, rope=rope)
    tab_spec = pl.BlockSpec((tm, LANES), lambda i, j: (j, 0))
    return pl.pallas_call(
        kern,
        out_shape=(jax.ShapeDtypeStruct((b, l, A_Q), MXU_DTYPE),
                   jax.ShapeDtypeStruct((b, l, 2 * A_KV), MXU_DTYPE),
                   jax.ShapeDtypeStruct((b, l, 2 * A_KV), MXU_DTYPE),
                   jax.ShapeDtypeStruct((b, l, HGRN_WIDTH), F32),
                   jax.ShapeDtypeStruct((b, l, 2 * HGRN_WIDTH), F32),
                   jax.ShapeDtypeStruct((b, l, HGRN_WIDTH), MXU_DTYPE),
                   jax.ShapeDtypeStruct((b, l, HGRN_WIDTH), MXU_DTYPE)),
        grid=(b, nblk),
        in_specs=[pl.BlockSpec((1, tm, d), lambda i, j: (i, j, 0)),
                  pl.BlockSpec((None, None, 6, d), lambda i, j: (layer, row_of_batch(i), 0, 0)),
                  _const_spec((1, d)),
                  _const_spec((d, EVEN_IN)),
                  _const_spec((1, A_Q)), _const_spec((1, A_KV)),
                  _const_spec((A_Q, A_Q)), _const_spec((A_KV, A_KV)),
                  tab_spec, tab_spec, _const_spec((1, HGRN_WIDTH))],
        out_specs=(pl.BlockSpec((1, tm, A_Q), lambda i, j: (i, j, 0)),
                   pl.BlockSpec((1, tm, 2 * A_KV), lambda i, j: (i, j, 0)),
                   pl.BlockSpec((1, tm, 2 * A_KV), lambda i, j: (i, j, 0)),
                   pl.BlockSpec((1, tm, HGRN_WIDTH), lambda i, j: (i, j, 0)),
                   pl.BlockSpec((1, tm, 2 * HGRN_WIDTH), lambda i, j: (i, j, 0)),
                   pl.BlockSpec((1, tm, HGRN_WIDTH), lambda i, j: (i, j, 0)),
                   pl.BlockSpec((1, tm, HGRN_WIDTH), lambda i, j: (i, j, 0))),
        compiler_params=_cparams(("parallel", "parallel")),
    )(x, mod4, ng, w, gq, gk, bdq, bdk, cs, sn, lb)


def _attn_kernel(*refs, window, tq, seq):
    if window:
        (q_ref, kp_ref, kc_ref, kn_ref, vp_ref, vc_ref, vn_ref, kx_ref, vx_ref, sink_ref, o_ref) = refs
    else:
        (q_ref, kx_ref, vx_ref, sink_ref, o_ref) = refs
    i = pl.program_id(1)
    n_ctx = kx_ref.shape[1]
    sub = WINDOW
    span = 3 * sub
    lane = lax.broadcasted_iota(jnp.int32, (2 * sub, LANES), 1)
    low = lane < HEAD_DIM
    if window:
        rr = lax.broadcasted_iota(jnp.int32, (sub, sub), 0)
        cc = lax.broadcasted_iota(jnp.int32, (sub, sub), 1)
    ones = jnp.ones((n_ctx + (span if window else 0), LANES), MXU_DTYPE)

    def scores(sb, g):
        rows = slice(sb * sub, (sb + 1) * sub)
        cols = slice(g * LANES, (g + 1) * LANES)
        if window:
            kfull = jnp.concatenate([kp_ref[0, :, cols], kc_ref[0, :, cols], kn_ref[0, :, cols]], axis=0)
            kk = jnp.concatenate([kfull[sb * sub:sb * sub + span], kx_ref[0, :, cols]], axis=0)
        else:
            kk = kx_ref[0, :, cols]
        q2 = jnp.concatenate([q_ref[0, rows, (2 * g) * LANES:(2 * g + 1) * LANES],
                              q_ref[0, rows, (2 * g + 1) * LANES:(2 * g + 2) * LANES]], axis=0)
        zero = jnp.zeros_like(q2)
        q4 = jnp.concatenate([jnp.where(low, q2, zero), jnp.where(low, zero, q2)], axis=0)
        return _dot_nt(q4, kk)

    def finish(sb, g, s):
        rows = slice(sb * sub, (sb + 1) * sub)
        cols = slice(g * LANES, (g + 1) * LANES)
        tiles = [s[:, t * LANES:(t + 1) * LANES] for t in range(s.shape[1] // LANES)]
        if window:
            base = i * tq + sb * sub
            ok_prev = (cc >= rr) & (base - sub + cc >= 0)
            ok_next = (cc <= rr) & (base + sub + cc < seq)
            tiles[0] = tiles[0] + jnp.concatenate([jnp.where(ok_prev, 0.0, NEG)] * 4, axis=0)
            tiles[2] = tiles[2] + jnp.concatenate([jnp.where(ok_next, 0.0, NEG)] * 4, axis=0)
            s = jnp.concatenate(tiles, axis=1)
            vfull = jnp.concatenate([vp_ref[0, :, cols], vc_ref[0, :, cols], vn_ref[0, :, cols]], axis=0)
            vv = jnp.concatenate([vfull[sb * sub:sb * sub + span], vx_ref[0, :, cols]], axis=0)
        else:
            vv = vx_ref[0, :, cols]
        tmax = functools.reduce(jnp.maximum, tiles)
        sink = sink_ref[g]
        m = jnp.maximum(jnp.max(tmax, axis=-1, keepdims=True), sink)
        p = jnp.exp2(s - m)
        pv = _dot(p, jnp.concatenate([vv, ones], axis=1))
        o = pv[:, :LANES] * (1.0 / (pv[:, LANES:] + jnp.exp2(sink - m)))
        o2 = jnp.where(low, o[:2 * sub], o[2 * sub:])
        o_ref[0, rows, (2 * g) * LANES:(2 * g + 1) * LANES] = o2[:sub].astype(o_ref.dtype)
        o_ref[0, rows, (2 * g + 1) * LANES:(2 * g + 2) * LANES] = o2[sub:].astype(o_ref.dtype)

    units = [(sb, g) for sb in range(tq // sub) for g in range(ATTN_KV_HEADS)]
    ahead = 1
    pending = [scores(*u) for u in units[:ahead]]
    for idx, unit in enumerate(units):
        if idx + ahead < len(units):
            pending.append(scores(*units[idx + ahead]))
        finish(*unit, pending.pop(0))


def _sink_rows(sink, tq):
    s = sink.astype(F32).reshape(ATTN_KV_HEADS, 4) * LOG2E
    order = jnp.array([0, 2, 1, 3])
    return jnp.repeat(s[:, order], tq, axis=1)[:, :, None]


def _attention(q, k, v, kx, vx, sink, *, window, tq):
    b, l, _ = q.shape
    n_ctx = kx.shape[1]
    nq = l // tq
    kern = functools.partial(_attn_kernel, window=window, tq=tq, seq=l)
    qspec = pl.BlockSpec((1, tq, A_Q), lambda i, j: (i, j, 0))
    xspec = pl.BlockSpec((1, n_ctx, 2 * A_KV), lambda i, j: (i, 0, 0))
    sspec = _const_spec((ATTN_KV_HEADS, 4 * WINDOW, 1))
    if window:
        r = tq // WINDOW
        prev = pl.BlockSpec((1, WINDOW, 2 * A_KV), lambda i, j: (i, jnp.maximum(j * r - 1, 0), 0))
        cur = pl.BlockSpec((1, tq, 2 * A_KV), lambda i, j: (i, j, 0))
        nxt = pl.BlockSpec((1, WINDOW, 2 * A_KV), lambda i, j: (i, jnp.minimum((j + 1) * r, nq * r - 1), 0))
        in_specs = [qspec, prev, cur, nxt, prev, cur, nxt, xspec, xspec, sspec]
        args = (q, k, k, k, v, v, v, kx, vx, _sink_rows(sink, WINDOW))
    else:
        in_specs = [qspec, xspec, xspec, sspec]
        args = (q, kx, vx, _sink_rows(sink, WINDOW))
    return pl.pallas_call(
        kern,
        out_shape=jax.ShapeDtypeStruct((b, l, A_Q), MXU_DTYPE),
        grid=(b, nq),
        in_specs=in_specs,
        out_specs=pl.BlockSpec((1, tq, A_Q), lambda i, j: (i, j, 0)),
        compiler_params=_cparams(("parallel", "parallel")),
    )(*args)


class _HgrnDirection:
    def __init__(self, q_ref, lf_ref, i_ref, st_ref, o_ref, *, rev, tb):
        self.q_ref, self.lf_ref, self.i_ref, self.st_ref, self.o_ref = q_ref, lf_ref, i_ref, st_ref, o_ref
        self.rev, self.tb, self.nc = rev, tb, tb // GLA_CHUNK

    def gates(self):
        tb, nc, c_len, width = self.tb, self.nc, GLA_CHUNK, HGRN_WIDTH
        self.qs3 = self.q_ref[0].reshape(nc, c_len, width)
        lf = self.lf_ref[0]
        self.kf3 = (1.0 - jnp.exp(lf)).reshape(nc, c_len, width)
        self.lf3 = lf.reshape(nc, c_len, width)
        self.iv = self.i_ref[0]
        r = lax.broadcasted_iota(jnp.int32, (tb, tb), 0)
        c = lax.broadcasted_iota(jnp.int32, (tb, tb), 1)
        same = (r // c_len) == (c // c_len)
        lower = same & (c <= r)
        self.keep = (same & (c >= r)) if self.rev else lower
        tri = _mx(jnp.where(lower, 1.0, 0.0))
        hi, lo = _split(lf)
        self.cum = (jnp.dot(tri, hi, preferred_element_type=F32)
                    + jnp.dot(tri, lo, preferred_element_type=F32)).reshape(nc, c_len, width)

    def first_matmuls(self):
        tb, nc, c_len, width = self.tb, self.nc, GLA_CHUNK, HGRN_WIDTH
        cum = self.cum
        if self.rev:
            w3 = cum[:, c_len - 1:c_len, :] - cum + self.lf3
            edge = w3[:, 0:1, :]
        else:
            w3 = cum
            edge = cum[:, c_len - 1:c_len, :]
        mid = w3[:, c_len // 2:c_len // 2 + 1, :]
        qa3 = self.qs3 * jnp.exp(w3 - mid)
        kb3 = self.kf3 * jnp.exp(mid - w3)
        qa = _mx(qa3.reshape(tb, width))
        kb = _mx(kb3.reshape(tb, width))
        self.qe = _mx((qa3 * jnp.exp(mid)).reshape(tb, width))
        ke = _mx((kb3 * jnp.exp(edge - mid)).reshape(tb, width))
        self.dec = jnp.exp(edge)
        iv = self.iv
        self.scores, self.kv = [], []
        for h in range(HGRN_HEADS):
            hs = slice(h * HGRN_DK, (h + 1) * HGRN_DK)
            self.scores.append(_dot_nt(qa[:, hs], kb[:, hs]))
            self.kv.append([_dot_tn(iv[ci * c_len:(ci + 1) * c_len, hs], ke[ci * c_len:(ci + 1) * c_len, hs])
                            for ci in range(nc)])

    def second_matmuls(self):
        nc, c_len = self.nc, GLA_CHUNK
        order = range(nc - 1, -1, -1) if self.rev else range(nc)
        for h in range(HGRN_HEADS):
            hs = slice(h * HGRN_DK, (h + 1) * HGRN_DK)
            o_intra = _dot(jnp.where(self.keep, self.scores[h], 0.0), self.iv[:, hs])
            st = self.st_ref[h]
            for ci in order:
                rows = slice(ci * c_len, (ci + 1) * c_len)
                self.o_ref[0, rows, hs] = o_intra[rows] + _dot_nt(self.qe[rows, hs], st)
                st = st * self.dec[ci, :, hs] + self.kv[h][ci]
            self.st_ref[h] = st


def _hgrn_kernel(qf_ref, ff_ref, if_ref, qb_ref, fb_ref, ib_ref, s0f_ref, s0b_ref,
                 of_ref, ob_ref, sff_ref, sfb_ref, st_ref, *, tb):
    j = pl.program_id(1)

    @pl.when(j == 0)
    def _():
        st_ref[0] = s0f_ref[0]
        st_ref[1] = s0b_ref[0]

    fw = _HgrnDirection(qf_ref, ff_ref, if_ref, st_ref.at[0], of_ref, rev=False, tb=tb)
    bw = _HgrnDirection(qb_ref, fb_ref, ib_ref, st_ref.at[1], ob_ref, rev=True, tb=tb)
    fw.gates()
    bw.gates()
    fw.first_matmuls()
    bw.first_matmuls()
    fw.second_matmuls()
    bw.second_matmuls()

    @pl.when(j == pl.num_programs(1) - 1)
    def _():
        sff_ref[0] = st_ref[0]
        sfb_ref[0] = st_ref[1]


def _hgrn_scan(hq, hl, hi, s0_fw, s0_bw, *, tb):
    b, l, _ = hq.shape
    nb = l // tb
    fcol = lambda cidx: pl.BlockSpec((1, tb, HGRN_WIDTH), lambda i, j: (i, j, cidx))
    bcol = lambda cidx: pl.BlockSpec((1, tb, HGRN_WIDTH), lambda i, j: (i, nb - 1 - j, cidx))
    st_shape = (HGRN_HEADS, HGRN_DK, HGRN_DK)
    st_spec = pl.BlockSpec((1,) + st_shape, lambda i, j: (i, 0, 0, 0))
    o_shape = jax.ShapeDtypeStruct((b, l, HGRN_WIDTH), F32)
    s_shape = jax.ShapeDtypeStruct((b,) + st_shape, F32)
    return pl.pallas_call(
        functools.partial(_hgrn_kernel, tb=tb),
        out_shape=(o_shape, o_shape, s_shape, s_shape),
        grid=(b, nb),
        in_specs=[fcol(0), fcol(0), fcol(0), bcol(0), bcol(1), bcol(0), st_spec, st_spec],
        out_specs=(fcol(0), bcol(0), st_spec, st_spec),
        scratch_shapes=[pltpu.VMEM((2,) + st_shape, F32)],
        compiler_params=_cparams(("parallel", "arbitrary")),
    )(hq, hl, hi, hq, hl, hi, s0_fw, s0_bw)


def _swiglu_residual(x1, mod_ref, ng2_ref, wi_ref, wd_ref):
    h2 = _mx(_norm_mod(x1, ng2_ref[...], mod_ref[3:4, :], mod_ref[4:5, :]))
    acc = None
    for lo in range(0, FFN_HIDDEN, FFN_CHUNK):
        gate = jnp.dot(h2, wi_ref[:, lo:lo + FFN_CHUNK], preferred_element_type=F32)
        up = jnp.dot(h2, wi_ref[:, FFN_HIDDEN + lo:FFN_HIDDEN + lo + FFN_CHUNK], preferred_element_type=F32)
        part = _dot(_silu(gate) * up, wd_ref[lo:lo + FFN_CHUNK, :])
        acc = part if acc is None else acc + part
    return x1 + mod_ref[5:6, :] * acc


def _out_kernel(*refs, n_mix, gated):
    mixes = [m_ref[0] for m_ref in refs[:n_mix]]
    rest = refs[n_mix:]
    if gated:
        of_ref, ob_ref, g_ref, og_ref = rest[:4]
        rest = rest[4:]
        o = of_ref[0] + ob_ref[0]
        og = og_ref[...]
        parts = []
        for h in range(HGRN_HEADS):
            oh = o[:, h * HGRN_DK:(h + 1) * HGRN_DK]
            parts.append(oh * lax.rsqrt(_mean_rows(oh * oh) + EPS) * og)
        mixes.append(_mx(jnp.concatenate(parts, axis=1) * g_ref[0].astype(F32)))
    x_ref, mod_ref, ng2_ref, wo_ref, wi_ref, wd_ref, o_ref = rest
    y, off = None, 0
    for m in mixes:
        w = m.shape[-1]
        part = jnp.dot(m, wo_ref[off:off + w, :], preferred_element_type=F32)
        y = part if y is None else y + part
        off += w
    x1 = x_ref[0] + mod_ref[2:3, :] * y
    o_ref[0] = _swiglu_residual(x1, mod_ref, ng2_ref, wi_ref, wd_ref)


def _out_layer(mixes, x, mod4, layer, row_of_batch, ng2, wo, wi, wd, *, tm, gated=None):
    b, l, d = x.shape
    row = lambda w: pl.BlockSpec((1, tm, w), lambda i, j: (i, j, 0))
    in_specs = [row(m.shape[-1]) for m in mixes]
    args = list(mixes)
    if gated is not None:
        in_specs += [row(HGRN_WIDTH), row(HGRN_WIDTH), row(HGRN_WIDTH), _const_spec((1, HGRN_DK))]
        args += list(gated)
    in_specs += [row(d),
                 pl.BlockSpec((None, None, 6, d), lambda i, j: (layer, row_of_batch(i), 0, 0)),
                 _const_spec((1, d)), _const_spec(wo.shape), _const_spec(wi.shape), _const_spec(wd.shape)]
    return pl.pallas_call(
        functools.partial(_out_kernel, n_mix=len(mixes), gated=gated is not None),
        out_shape=jax.ShapeDtypeStruct((b, l, d), F32),
        grid=(b, l // tm),
        in_specs=in_specs,
        out_specs=row(d),
        compiler_params=_cparams(("parallel", "parallel")),
    )(*args, x, mod4, ng2, wo, wi, wd)


def _rope_ret(u, cs, sn):
    parts = []
    for h in range(RET_HEADS):
        x1 = u[:, h * RET_DK:h * RET_DK + LANES]
        x2 = u[:, h * RET_DK + LANES:(h + 1) * RET_DK]
        parts += [x1 * cs - x2 * sn, x2 * cs + x1 * sn]
    return jnp.concatenate(parts, axis=1)


def _ret_log_gamma(h, rev):
    hh = RET_HEADS - 1 - h if rev else h
    return math.log(1.0 - 2.0 ** (-5.0 - hh))


def _ret_weights(n, lg, rev):
    pos = lax.broadcasted_iota(jnp.int32, (n, 1), 0).astype(F32)
    if rev:
        return jnp.exp(lg * pos), jnp.exp(lg * (n - pos))
    return jnp.exp(lg * (n - 1.0 - pos)), jnp.exp(lg * (pos + 1.0))


def _ret_tables(n, rev):
    pos = np.arange(n, dtype=np.float64)[:, None]
    out = np.zeros((RET_HEADS, 2, n, LANES))
    for h in range(RET_HEADS):
        lg = _ret_log_gamma(h, rev)
        out[h, 0] = np.exp(lg * pos) if rev else np.exp(lg * (n - 1.0 - pos))
        out[h, 1] = np.exp(lg * (n - pos)) if rev else np.exp(lg * (pos + 1.0))
    return jnp.asarray(out, F32)


def _ret_init_state(kc_ref, vc_ref, s_ref, rev):
    lc = kc_ref.shape[1]
    for h in range(RET_HEADS):
        kw, _ = _ret_weights(lc, _ret_log_gamma(h, rev), rev)
        kc = kc_ref[0, :, h * RET_DK:(h + 1) * RET_DK].astype(F32) * kw
        s_ref[h] = _dot_tn(kc, vc_ref[0, :, h * RET_DV:(h + 1) * RET_DV])


def _proj_odd_kernel(x_ref, mod_ref, ng_ref, w_ref, cs_ref, sn_ref, kc_ref, vc_ref, wt_ref,
                     q_ref, k_ref, v_ref, g_ref, ob_ref, s_ref, *, chunk):
    @pl.when(pl.program_id(1) == 0)
    def _():
        _ret_init_state(kc_ref, vc_ref, s_ref, True)

    h = _mx(_norm_mod(x_ref[0], ng_ref[...], mod_ref[0:1, :], mod_ref[1:2, :]))
    cs, sn = cs_ref[...], sn_ref[...]
    g = jnp.dot(h, w_ref[:, 2 * ODD_Q + ODD_V:], preferred_element_type=F32)
    g_ref[0] = _silu(g).astype(g_ref.dtype)
    q = _mx(_rope_ret(jnp.dot(h, w_ref[:, :ODD_Q], preferred_element_type=F32), cs, sn))
    q_ref[0] = q
    k = _rope_ret(jnp.dot(h, w_ref[:, ODD_Q:2 * ODD_Q], preferred_element_type=F32), cs, sn) * RET_DK ** -0.5
    k_ref[0] = k.astype(k_ref.dtype)
    v = _mx(jnp.dot(h, w_ref[:, 2 * ODD_Q:2 * ODD_Q + ODD_V], preferred_element_type=F32))
    v_ref[0] = v
    for ci in range(x_ref.shape[1] // chunk - 1, -1, -1):
        rows = slice(ci * chunk, (ci + 1) * chunk)
        for hd in range(RET_HEADS):
            ks, vs = slice(hd * RET_DK, (hd + 1) * RET_DK), slice(hd * RET_DV, (hd + 1) * RET_DV)
            s = s_ref[hd]
            ob_ref[0, rows, vs] = _scale_rows(_dot(q[rows, ks], s), wt_ref[hd, 1]).astype(ob_ref.dtype)
            s_ref[hd] = (s * math.exp(_ret_log_gamma(hd, True) * chunk)
                         + _dot_tn(_scale_rows(k[rows, ks], wt_ref[hd, 0]), v[rows, vs]))


def _proj_odd(x, mod4, layer, ng, w, cs, sn, kc, vc, *, tm, chunk):
    b, l, d = x.shape
    lc = kc.shape[1]
    nb = l // tm
    row = lambda wd: pl.BlockSpec((1, tm, wd), lambda i, j: (i, nb - 1 - j, 0))
    tab = pl.BlockSpec((tm, LANES), lambda i, j: (nb - 1 - j, 0))
    ctx = lambda wd: pl.BlockSpec((1, lc, wd), lambda i, j: (i, 0, 0))
    wide = jax.ShapeDtypeStruct((b, l, ODD_V), MXU_DTYPE)
    narrow = jax.ShapeDtypeStruct((b, l, ODD_Q), MXU_DTYPE)
    return pl.pallas_call(
        functools.partial(_proj_odd_kernel, chunk=chunk),
        out_shape=(narrow, narrow, wide, wide, wide),
        grid=(b, nb),
        in_specs=[row(d),
                  pl.BlockSpec((None, None, 6, d), lambda i, j: (layer, i, 0, 0)),
                  _const_spec((1, d)), _const_spec((d, ODD_IN)), tab, tab, ctx(ODD_Q), ctx(ODD_V),
                  _const_spec((RET_HEADS, 2, chunk, LANES))],
        out_specs=(row(ODD_Q), row(ODD_Q), row(ODD_V), row(ODD_V), row(ODD_V)),
        scratch_shapes=[pltpu.VMEM((RET_HEADS, RET_DK, RET_DV), F32)],
        compiler_params=_cparams(("parallel", "arbitrary")),
    )(x, mod4, ng, w, cs, sn, kc, vc, _ret_tables(chunk, True))


def _proj_ctx_kv_kernel(x_ref, mod_ref, ng_ref, w_ref, k_ref, v_ref):
    h = _mx(_norm_mod(x_ref[0], ng_ref[...], mod_ref[0:1, :], mod_ref[1:2, :]))
    k = jnp.dot(h, w_ref[:, :ODD_Q], preferred_element_type=F32)
    k_ref[0] = (k * RET_DK ** -0.5).astype(k_ref.dtype)
    v_ref[0] = jnp.dot(h, w_ref[:, ODD_Q:], preferred_element_type=F32).astype(v_ref.dtype)


def _proj_ctx_kv(ctx, mod4, layer, ctx_row, ng, w_kv):
    b, lc, d = ctx.shape
    return pl.pallas_call(
        _proj_ctx_kv_kernel,
        out_shape=(jax.ShapeDtypeStruct((b, lc, ODD_Q), MXU_DTYPE),
                   jax.ShapeDtypeStruct((b, lc, ODD_V), MXU_DTYPE)),
        grid=(b,),
        in_specs=[pl.BlockSpec((1, lc, d), lambda i: (i, 0, 0)),
                  pl.BlockSpec((None, None, 6, d), lambda i: (layer, ctx_row, 0, 0)),
                  _const_spec((1, d)), _const_spec((d, ODD_Q + ODD_V))],
        out_specs=(pl.BlockSpec((1, lc, ODD_Q), lambda i: (i, 0, 0)),
                   pl.BlockSpec((1, lc, ODD_V), lambda i: (i, 0, 0))),
        compiler_params=_cparams(("parallel",)),
    )(ctx, mod4, ng, w_kv)


def _ret_kernel(q_ref, k_ref, v_ref, kc_ref, vc_ref, dm_ref, wt_ref, ob_ref, sg_ref, o_ref, s_ref, *, chunk):
    @pl.when(pl.program_id(1) == 0)
    def _():
        _ret_init_state(kc_ref, vc_ref, s_ref, False)

    state = [s_ref[h] for h in range(RET_HEADS)]
    staged = []
    for ci in range(q_ref.shape[1] // chunk):
        rows = slice(ci * chunk, (ci + 1) * chunk)
        for h in range(RET_HEADS):
            q = q_ref[0, rows, h * RET_DK:(h + 1) * RET_DK]
            k = k_ref[0, rows, h * RET_DK:(h + 1) * RET_DK]
            v = v_ref[0, rows, h * RET_DV:(h + 1) * RET_DV]
            o = _scale_rows(_dot(q, state[h]), wt_ref[h, 1])
            sc = _dot_nt(q, k)
            state[h] = (state[h] * math.exp(_ret_log_gamma(h, False) * chunk)
                        + _dot_tn(_scale_rows(k.astype(F32), wt_ref[h, 0]), v))
            staged.append((rows, h, o, sc, v))
    for h in range(RET_HEADS):
        s_ref[h] = state[h]
    for rows, h, o, sc, v in staged:
        vs = slice(h * RET_DV, (h + 1) * RET_DV)
        o = o + _dot(sc * dm_ref[h], v) + ob_ref[0, rows, vs].astype(F32)
        o = o * lax.rsqrt(_mean_rows(o * o) + EPS)
        o_ref[0, rows, vs] = (o * sg_ref[0, rows, vs].astype(F32)).astype(o_ref.dtype)


def _ret_decay_matrix(tb):
    rel = np.arange(tb)[:, None] - np.arange(tb)[None, :]
    out = np.zeros((RET_HEADS, tb, tb), np.float64)
    for h in range(RET_HEADS):
        fw = np.exp(_ret_log_gamma(h, False) * np.maximum(rel, 0))
        bw = np.exp(_ret_log_gamma(h, True) * np.maximum(-rel, 0))
        out[h] = np.where(rel > 0, fw, np.where(rel < 0, bw, 2.0))
    return jnp.asarray(out, F32)


def _ret_scan(q, k, v, kc, vc, o_bw, sg, *, tb, chunk):
    b, l, _ = q.shape
    lc = kc.shape[1]
    row = lambda w: pl.BlockSpec((1, tb, w), lambda i, j: (i, j, 0))
    ctx = lambda w: pl.BlockSpec((1, lc, w), lambda i, j: (i, 0, 0))
    return pl.pallas_call(
        functools.partial(_ret_kernel, chunk=chunk),
        out_shape=jax.ShapeDtypeStruct((b, l, ODD_V), MXU_DTYPE),
        grid=(b, l // tb),
        in_specs=[row(ODD_Q), row(ODD_Q), row(ODD_V), ctx(ODD_Q), ctx(ODD_V),
                  _const_spec((RET_HEADS, chunk, chunk)), _const_spec((RET_HEADS, 2, chunk, LANES)),
                  row(ODD_V), row(ODD_V)],
        out_specs=row(ODD_V),
        scratch_shapes=[pltpu.VMEM((RET_HEADS, RET_DK, RET_DV), F32)],
        compiler_params=_cparams(("parallel", "arbitrary")),
    )(q, k, v, kc, vc, _ret_decay_matrix(chunk), _ret_tables(chunk, False), o_bw, sg)


def _axial_tables(n_tok):
    t = np.arange(n_tok)
    n_freq = HEAD_DIM // 4
    inv = ROPE_BASE ** (-np.arange(n_freq, dtype=np.float64) / n_freq)
    ang = np.concatenate([(t // GRID_W)[:, None] * inv, (t % GRID_W)[:, None] * inv], axis=-1)
    cos, sin = np.cos(ang), np.sin(ang)
    cs = np.tile(np.concatenate([cos, cos], axis=-1), (1, LANES // HEAD_DIM))
    sn = np.tile(np.concatenate([-sin, sin], axis=-1), (1, LANES // HEAD_DIM))
    return jnp.asarray(cs, F32), jnp.asarray(sn, F32)


def _retention_tables(n_tok):
    theta = 1.0 / (RET_BASE ** np.linspace(0.0, 1.0, RET_DK // 2))
    ang = np.arange(n_tok, dtype=np.float64)[:, None] * theta
    return jnp.asarray(np.cos(ang), F32), jnp.asarray(np.sin(ang), F32)


def _block_diag_mean(width, head):
    idx = np.arange(width) // head
    return jnp.asarray((idx[:, None] == idx[None, :]) / head, MXU_DTYPE)


def _pick(n, pref):
    while n % pref:
        pref //= 2
    return pref


def kernel(x, c, ctx, c_ctx, mod_w, mod_b, norm_g, ffn_w_in, ffn_w_out, even_w_in, even_w_out,
           attn_qk_norm_g, attn_sink, hgrn_out_norm_g, hgrn_lb, odd_w_in, odd_w_out):
    n_b, n_tok, d = x.shape
    n_ctx = ctx.shape[1]
    depth = mod_w.shape[0]
    assert depth == 2 and d == D_MODEL

    rows = -(-(n_b + 1) // 8) * 8
    cond = jnp.zeros((rows, d), F32).at[:n_b].set(c).at[n_b].set(c_ctx)
    mod4 = _modulation(cond, mod_w, mod_b).reshape(depth, rows, 6, d)
    lat_row = lambda i: i
    ctx_row = lambda i: n_b

    ng1, ng2 = norm_g[0, 0][None, :], norm_g[0, 1][None, :]
    w_in = _mx(even_w_in[0])
    gq = jnp.tile(attn_qk_norm_g[0, 0], ATTN_HEADS)[None, :]
    gk = jnp.tile(attn_qk_norm_g[0, 1], ATTN_KV_HEADS)[None, :]
    bdq, bdk = _block_diag_mean(A_Q, HEAD_DIM), _block_diag_mean(A_KV, HEAD_DIM)
    cs_a, sn_a = _axial_tables(n_tok)
    lower = jnp.cumsum(jax.nn.softmax(hgrn_lb.astype(F32), axis=0), axis=0)[0][None, :]
    og = hgrn_out_norm_g[0][None, :]
    wo = _mx(even_w_out[0])
    wi, wd = _mx(ffn_w_in[0]), _mx(ffn_w_out[0])

    tm_c = _pick(n_ctx, 256)
    tm_l = _pick(n_tok, 512)
    ones_c = jnp.ones((n_ctx, LANES), F32)
    qc, kc, vc, hqc, hlc, hic, hgc = _proj_even(ctx, mod4, 0, ctx_row, ng1, w_in, gq, gk, bdq, bdk, ones_c, ones_c,
                                                lower, rope=False, tm=tm_c)
    ql, kl, vl, hql, hll, hil, hgl = _proj_even(x, mod4, 0, lat_row, ng1, w_in, gq, gk, bdq, bdk, cs_a, sn_a,
                                                lower, rope=True, tm=tm_l)

    a_ctx = _attention(qc, None, None, kc, vc, attn_sink[0], window=False, tq=WINDOW)
    a_lat = _attention(ql, kl, vl, kc, vc, attn_sink[0], window=True, tq=_pick(n_tok, 4 * WINDOW))

    zeros = jnp.zeros((n_b, HGRN_HEADS, HGRN_DK, HGRN_DK), F32)
    tb_c, tb_l = _pick(n_ctx, 256), _pick(n_tok, 256)
    o_fw_c, o_bw_c, s_fw, s_bw = _hgrn_scan(hqc, hlc, hic, zeros, zeros, tb=tb_c)
    o_fw, o_bw, _, _ = _hgrn_scan(hql, hll, hil, s_fw, s_bw, tb=tb_l)

    ctx1 = _out_layer([a_ctx], ctx, mod4, 0, ctx_row, ng2, wo, wi, wd, tm=tm_c,
                      gated=(o_fw_c, o_bw_c, hgc, og))
    x1 = _out_layer([a_lat], x, mod4, 0, lat_row, ng2, wo, wi, wd, tm=tm_l,
                    gated=(o_fw, o_bw, hgl, og))

    ng1, ng2 = norm_g[1, 0][None, :], norm_g[1, 1][None, :]
    w_in = _mx(odd_w_in[0])
    cs_r, sn_r = _retention_tables(n_tok)
    wo = _mx(odd_w_out[0])
    wi, wd = _mx(ffn_w_in[1]), _mx(ffn_w_out[1])

    kc, vc = _proj_ctx_kv(ctx1, mod4, 1, n_b, ng1, w_in[:, ODD_Q:2 * ODD_Q + ODD_V])
    tb_r = _pick(n_tok, 256)
    q, k, v, sg, o_bw = _proj_odd(x1, mod4, 1, ng1, w_in, cs_r, sn_r, kc, vc, tm=_pick(n_tok, 512), chunk=tb_r)
    mix = _ret_scan(q, k, v, kc, vc, o_bw, sg, tb=_pick(n_tok, 512), chunk=tb_r)
    return _out_layer([mix], x1, mod4, 1, lat_row, ng2, wo, wi, wd, tm=_pick(n_tok, 512))
```

```python
import functools
import math

import numpy as np
import jax
import jax.numpy as jnp
from jax import lax
from jax.experimental import pallas as pl
from jax.experimental.pallas import tpu as pltpu

F32 = jnp.float32
MXU_DTYPE = jnp.bfloat16

D_MODEL = 1024
GRID_W = 64
HEAD_DIM = 64
ATTN_HEADS = 8
ATTN_KV_HEADS = 2
WINDOW = 128
ROPE_BASE = 10000.0
HGRN_HEADS = 4
HGRN_DK = 128
HGRN_WIDTH = HGRN_HEADS * HGRN_DK
GLA_CHUNK = 64
RET_HEADS = 4
RET_DK = 256
RET_DV = 512
RET_BASE = 10000.0
FFN_HIDDEN = 2816
A_Q = ATTN_HEADS * HEAD_DIM
A_KV = ATTN_KV_HEADS * HEAD_DIM
EVEN_IN = A_Q + 2 * A_KV + 5 * HGRN_WIDTH
HG_OFF = A_Q + 2 * A_KV
ODD_Q = RET_HEADS * RET_DK
ODD_V = RET_HEADS * RET_DV
ODD_IN = 2 * ODD_Q + 2 * ODD_V
EPS = 1e-6
NEG = -1e30
LOG2E = 1.4426950408889634

LANES = 128
MXU_TILE = 256
FFN_CHUNK = MXU_TILE
VMEM_LIMIT = 56 * 1024 * 1024


def _mx(a):
    return a.astype(MXU_DTYPE)


def _dot(a, b):
    return jnp.dot(_mx(a), _mx(b), preferred_element_type=F32)


def _dot_nt(a, b):
    return lax.dot_general(_mx(a), _mx(b), (((1,), (1,)), ((), ())), preferred_element_type=F32)


def _dot_tn(a, b):
    return lax.dot_general(_mx(a), _mx(b), (((0,), (0,)), ((), ())), preferred_element_type=F32)


def _split(x):
    hi = _mx(x)
    lo = _mx(x - hi.astype(F32))
    return hi, lo


def _sigmoid(x):
    return 1.0 / (1.0 + jnp.exp(-x))


def _silu(x):
    return x * _sigmoid(x)


def _norm_mod(x, g, shift, scale):
    y = x * lax.rsqrt(_mean_rows(x * x) + EPS) * g
    return y * (1.0 + scale) + shift


def _lane_tiles(x):
    return [x[:, t * LANES:(t + 1) * LANES] for t in range(x.shape[1] // LANES)]


def _scale_rows(x, tab):
    return jnp.concatenate([t * tab for t in _lane_tiles(x)], axis=1)


def _mean_rows(x):
    return jnp.sum(functools.reduce(jnp.add, _lane_tiles(x)), axis=-1, keepdims=True) * (1.0 / x.shape[1])


def _cparams(sem):
    return pltpu.CompilerParams(dimension_semantics=sem, vmem_limit_bytes=VMEM_LIMIT)


def _const_spec(shape):
    nd = len(shape)
    return pl.BlockSpec(shape, lambda *_: (0,) * nd, pipeline_mode=pl.Buffered(1))


def _mod_kernel(c_ref, w_ref, b_ref, o_ref):
    s = _silu(c_ref[...])
    o_ref[0] = jnp.dot(s, w_ref[0], preferred_element_type=F32,
                       precision=lax.Precision.HIGHEST) + b_ref[0]


def _modulation(cond, mod_w, mod_b):
    depth, d, n = mod_w.shape
    rows = cond.shape[0]
    tn = 1536
    return pl.pallas_call(
        _mod_kernel,
        out_shape=jax.ShapeDtypeStruct((depth, rows, n), F32),
        grid=(depth, n // tn),
        in_specs=[pl.BlockSpec((rows, d), lambda l, j: (0, 0)),
                  pl.BlockSpec((1, d, tn), lambda l, j: (l, 0, j)),
                  pl.BlockSpec((1, 1, tn), lambda l, j: (l, 0, j))],
        out_specs=pl.BlockSpec((1, rows, tn), lambda l, j: (l, 0, j)),
        compiler_params=_cparams(("arbitrary", "arbitrary")),
    )(cond, mod_w, mod_b.reshape(depth, 1, n))


def _head_rms(u, bd, gain):
    hi, lo = _split(u * u)
    width = u.shape[1]
    step = min(width, MXU_TILE)
    ms = jnp.concatenate(
        [jnp.dot(hi[:, c:c + step], bd[c:c + step, c:c + step], preferred_element_type=F32)
         + jnp.dot(lo[:, c:c + step], bd[c:c + step, c:c + step], preferred_element_type=F32)
         for c in range(0, width, step)], axis=1)
    return u * lax.rsqrt(ms + EPS) * gain


def _rope_half_swap(u):
    lane = lax.broadcasted_iota(jnp.int32, u.shape, 1)
    first = (lane % HEAD_DIM) < (HEAD_DIM // 2)
    return jnp.where(first, pltpu.roll(u, LANES - HEAD_DIM // 2, 1), pltpu.roll(u, HEAD_DIM // 2, 1))


def _dup_halves(u):
    lane = lax.broadcasted_iota(jnp.int32, u.shape, 1)
    low = lane < HEAD_DIM
    swapped = pltpu.roll(u, HEAD_DIM, 1)
    return jnp.concatenate([jnp.where(low, u, swapped), jnp.where(low, swapped, u)], axis=1)


def _proj_even_kernel(x_ref, mod_ref, ng_ref, w_ref, gq_ref, gk_ref, bdq_ref, bdk_ref, cs_ref, sn_ref, lb_ref,
                      q_ref, k_ref, v_ref, hq_ref, hl_ref, hi_ref, hg_ref, *, rope, n_split):
    lb = lb_ref[...]
    col = lambda c: w_ref[:, HG_OFF + c * HGRN_WIDTH:HG_OFF + (c + 1) * HGRN_WIDTH]
    log_gate = lambda raw: jnp.log(lb + (1.0 - lb) * _sigmoid(raw))
    size = x_ref.shape[1] // n_split
    for grp in range(n_split):
        rows = slice(grp * size, (grp + 1) * size)
        h = _mx(_norm_mod(x_ref[0, rows, :], ng_ref[...], mod_ref[0:1, :], mod_ref[1:2, :]))
        hl_ref[0, rows, :HGRN_WIDTH] = log_gate(jnp.dot(h, col(1), preferred_element_type=F32))
        hl_ref[0, rows, HGRN_WIDTH:] = log_gate(jnp.dot(h, col(2), preferred_element_type=F32))
        qkv = jnp.dot(h, w_ref[:, :HG_OFF], preferred_element_type=F32)
        hg_ref[0, rows, :] = _silu(jnp.dot(h, col(4), preferred_element_type=F32)).astype(hg_ref.dtype)
        q = _head_rms(qkv[:, :A_Q], bdq_ref[...], gq_ref[...])
        k = _head_rms(qkv[:, A_Q:A_Q + A_KV], bdk_ref[...], gk_ref[...])
        hq_ref[0, rows, :] = _silu(jnp.dot(h, col(0), preferred_element_type=F32))
        hi_ref[0, rows, :] = jnp.dot(h, col(3), preferred_element_type=F32).astype(hi_ref.dtype)
        v = qkv[:, A_Q + A_KV:HG_OFF]
        if rope:
            cs, sn = cs_ref[rows, :], sn_ref[rows, :]
            q = jnp.concatenate(
                [q[:, c * LANES:(c + 1) * LANES] * cs + _rope_half_swap(q[:, c * LANES:(c + 1) * LANES]) * sn
                 for c in range(A_Q // LANES)], axis=1)
            k = k * cs + _rope_half_swap(k) * sn
        q_ref[0, rows, :] = (q * (HEAD_DIM ** -0.5 * LOG2E)).astype(q_ref.dtype)
        k_ref[0, rows, :] = _dup_halves(k).astype(k_ref.dtype)
        v_ref[0, rows, :] = _dup_halves(v).astype(v_ref.dtype)


def _proj_even(x, mod4, layer, row_of_batch, ng, w, gq, gk, bdq, bdk, cs, sn, lb, *, rope, tm, n_split=1):
    b, l, d = x.shape
    nblk = l // tm
    kern = functools.partial(_proj_even_kernel, rope=rope, n_split=n_split)
    tab_spec = pl.BlockSpec((tm, LANES), lambda i, j: (j, 0))
    return pl.pallas_call(
        kern,
        out_shape=(jax.ShapeDtypeStruct((b, l, A_Q), MXU_DTYPE),
                   jax.ShapeDtypeStruct((b, l, 2 * A_KV), MXU_DTYPE),
                   jax.ShapeDtypeStruct((b, l, 2 * A_KV), MXU_DTYPE),
                   jax.ShapeDtypeStruct((b, l, HGRN_WIDTH), F32),
                   jax.ShapeDtypeStruct((b, l, 2 * HGRN_WIDTH), F32),
                   jax.ShapeDtypeStruct((b, l, HGRN_WIDTH), MXU_DTYPE),
                   jax.ShapeDtypeStruct((b, l, HGRN_WIDTH), MXU_DTYPE)),
        grid=(b, nblk),
        in_specs=[pl.BlockSpec((1, tm, d), lambda i, j: (i, j, 0)),
                  pl.BlockSpec((None, None, 6, d), lambda i, j: (layer, row_of_batch(i), 0, 0)),
                  _const_spec((1, d)),
                  _const_spec((d, EVEN_IN)),
                  _const_spec((1, A_Q)), _const_spec((1, A_KV)),
                  _const_spec((A_Q, A_Q)), _const_spec((A_KV, A_KV)),
                  tab_spec, tab_spec, _const_spec((1, HGRN_WIDTH))],
        out_specs=(pl.BlockSpec((1, tm, A_Q), lambda i, j: (i, j, 0)),
                   pl.BlockSpec((1, tm, 2 * A_KV), lambda i, j: (i, j, 0)),
                   pl.BlockSpec((1, tm, 2 * A_KV), lambda i, j: (i, j, 0)),
                   pl.BlockSpec((1, tm, HGRN_WIDTH), lambda i, j: (i, j, 0)),
                   pl.BlockSpec((1, tm, 2 * HGRN_WIDTH), lambda i, j: (i, j, 0)),
                   pl.BlockSpec((1, tm, HGRN_WIDTH), lambda i, j: (i, j, 0)),
                   pl.BlockSpec((1, tm, HGRN_WIDTH), lambda i, j: (i, j, 0))),
        compiler_params=_cparams(("parallel", "parallel")),
    )(x, mod4, ng, w, gq, gk, bdq, bdk, cs, sn, lb)


def _attn_kernel(*refs, window, tq, seq):
    if window:
        (q_ref, kp_ref, kc_ref, kn_ref, vp_ref, vc_ref, vn_ref, kx_ref, vx_ref, sink_ref, o_ref) = refs
    else:
        (q_ref, kx_ref, vx_ref, sink_ref, o_ref) = refs
    i = pl.program_id(1)
    n_ctx = kx_ref.shape[1]
    sub = WINDOW
    span = 3 * sub
    lane = lax.broadcasted_iota(jnp.int32, (2 * sub, LANES), 1)
    low = lane < HEAD_DIM
    if window:
        rr = lax.broadcasted_iota(jnp.int32, (sub, sub), 0)
        cc = lax.broadcasted_iota(jnp.int32, (sub, sub), 1)
    ones = jnp.ones((n_ctx + (span if window else 0), LANES), MXU_DTYPE)

    def scores(sb, g):
        rows = slice(sb * sub, (sb + 1) * sub)
        cols = slice(g * LANES, (g + 1) * LANES)
        if window:
            kfull = jnp.concatenate([kp_ref[0, :, cols], kc_ref[0, :, cols], kn_ref[0, :, cols]], axis=0)
            kk = jnp.concatenate([kfull[sb * sub:sb * sub + span], kx_ref[0, :, cols]], axis=0)
        else:
            kk = kx_ref[0, :, cols]
        q2 = jnp.concatenate([q_ref[0, rows, (2 * g) * LANES:(2 * g + 1) * LANES],
                              q_ref[0, rows, (2 * g + 1) * LANES:(2 * g + 2) * LANES]], axis=0)
        zero = jnp.zeros_like(q2)
        q4 = jnp.concatenate([jnp.where(low, q2, zero), jnp.where(low, zero, q2)], axis=0)
        return _dot_nt(q4, kk)

    def finish(sb, g, s):
        rows = slice(sb * sub, (sb + 1) * sub)
        cols = slice(g * LANES, (g + 1) * LANES)
        tiles = [s[:, t * LANES:(t + 1) * LANES] for t in range(s.shape[1] // LANES)]
        if window:
            base = i * tq + sb * sub
            ok_prev = (cc >= rr) & (base - sub + cc >= 0)
            ok_next = (cc <= rr) & (base + sub + cc < seq)
            tiles[0] = tiles[0] + jnp.concatenate([jnp.where(ok_prev, 0.0, NEG)] * 4, axis=0)
            tiles[2] = tiles[2] + jnp.concatenate([jnp.where(ok_next, 0.0, NEG)] * 4, axis=0)
            s = jnp.concatenate(tiles, axis=1)
            vfull = jnp.concatenate([vp_ref[0, :, cols], vc_ref[0, :, cols], vn_ref[0, :, cols]], axis=0)
            vv = jnp.concatenate([vfull[sb * sub:sb * sub + span], vx_ref[0, :, cols]], axis=0)
        else:
            vv = vx_ref[0, :, cols]
        tmax = functools.reduce(jnp.maximum, tiles)
        sink = sink_ref[g]
        m = jnp.maximum(jnp.max(tmax, axis=-1, keepdims=True), sink)
        p = jnp.exp2(s - m)
        pv = _dot(p, jnp.concatenate([vv, ones], axis=1))
        o = pv[:, :LANES] * (1.0 / (pv[:, LANES:] + jnp.exp2(sink - m)))
        o2 = jnp.where(low, o[:2 * sub], o[2 * sub:])
        o_ref[0, rows, (2 * g) * LANES:(2 * g + 1) * LANES] = o2[:sub].astype(o_ref.dtype)
        o_ref[0, rows, (2 * g + 1) * LANES:(2 * g + 2) * LANES] = o2[sub:].astype(o_ref.dtype)

    units = [(sb, g) for sb in range(tq // sub) for g in range(ATTN_KV_HEADS)]
    ahead = 1
    pending = [scores(*u) for u in units[:ahead]]
    for idx, unit in enumerate(units):
        if idx + ahead < len(units):
            pending.append(scores(*units[idx + ahead]))
        finish(*unit, pending.pop(0))


def _sink_rows(sink, tq):
    s = sink.astype(F32).reshape(ATTN_KV_HEADS, 4) * LOG2E
    order = jnp.array([0, 2, 1, 3])
    return jnp.repeat(s[:, order], tq, axis=1)[:, :, None]


def _attention(q, k, v, kx, vx, sink, *, window, tq):
    b, l, _ = q.shape
    n_ctx = kx.shape[1]
    nq = l // tq
    kern = functools.partial(_attn_kernel, window=window, tq=tq, seq=l)
    qspec = pl.BlockSpec((1, tq, A_Q), lambda i, j: (i, j, 0))
    xspec = pl.BlockSpec((1, n_ctx, 2 * A_KV), lambda i, j: (i, 0, 0))
    sspec = _const_spec((ATTN_KV_HEADS, 4 * WINDOW, 1))
    if window:
        r = tq // WINDOW
        prev = pl.BlockSpec((1, WINDOW, 2 * A_KV), lambda i, j: (i, jnp.maximum(j * r - 1, 0), 0))
        cur = pl.BlockSpec((1, tq, 2 * A_KV), lambda i, j: (i, j, 0))
        nxt = pl.BlockSpec((1, WINDOW, 2 * A_KV), lambda i, j: (i, jnp.minimum((j + 1) * r, nq * r - 1), 0))
        in_specs = [qspec, prev, cur, nxt, prev, cur, nxt, xspec, xspec, sspec]
        args = (q, k, k, k, v, v, v, kx, vx, _sink_rows(sink, WINDOW))
    else:
        in_specs = [qspec, xspec, xspec, sspec]
        args = (q, kx, vx, _sink_rows(sink, WINDOW))
    return pl.pallas_call(
        kern,
        out_shape=jax.ShapeDtypeStruct((b, l, A_Q), MXU_DTYPE),
        grid=(b, nq),
        in_specs=in_specs,
        out_specs=pl.BlockSpec((1, tq, A_Q), lambda i, j: (i, j, 0)),
        compiler_params=_cparams(("parallel", "parallel")),
    )(*args)


class _HgrnDirection:
    def __init__(self, q_ref, lf_ref, i_ref, st_ref, o_ref, *, rev, tb):
        self.q_ref, self.lf_ref, self.i_ref, self.st_ref, self.o_ref = q_ref, lf_ref, i_ref, st_ref, o_ref
        self.rev, self.tb, self.nc = rev, tb, tb // GLA_CHUNK

    def gates(self):
        tb, nc, c_len, width = self.tb, self.nc, GLA_CHUNK, HGRN_WIDTH
        self.qs3 = self.q_ref[0].reshape(nc, c_len, width)
        lf = self.lf_ref[0]
        self.kf3 = (1.0 - jnp.exp(lf)).reshape(nc, c_len, width)
        self.lf3 = lf.reshape(nc, c_len, width)
        self.iv = self.i_ref[0]
        r = lax.broadcasted_iota(jnp.int32, (tb, tb), 0)
        c = lax.broadcasted_iota(jnp.int32, (tb, tb), 1)
        same = (r // c_len) == (c // c_len)
        lower = same & (c <= r)
        self.keep = (same & (c >= r)) if self.rev else lower
        tri = _mx(jnp.where(lower, 1.0, 0.0))
        hi, lo = _split(lf)
        self.cum = (jnp.dot(tri, hi, preferred_element_type=F32)
                    + jnp.dot(tri, lo, preferred_element_type=F32)).reshape(nc, c_len, width)

    def first_matmuls(self):
        tb, nc, c_len, width = self.tb, self.nc, GLA_CHUNK, HGRN_WIDTH
        cum = self.cum
        if self.rev:
            w3 = cum[:, c_len - 1:c_len, :] - cum + self.lf3
            edge = w3[:, 0:1, :]
        else:
            w3 = cum
            edge = cum[:, c_len - 1:c_len, :]
        mid = w3[:, c_len // 2:c_len // 2 + 1, :]
        qa3 = self.qs3 * jnp.exp(w3 - mid)
        kb3 = self.kf3 * jnp.exp(mid - w3)
        qa = _mx(qa3.reshape(tb, width))
        kb = _mx(kb3.reshape(tb, width))
        self.qe = _mx((qa3 * jnp.exp(mid)).reshape(tb, width))
        ke = _mx((kb3 * jnp.exp(edge - mid)).reshape(tb, width))
        self.dec = jnp.exp(edge)
        iv = self.iv
        self.scores, self.kv = [], []
        for h in range(HGRN_HEADS):
            hs = slice(h * HGRN_DK, (h + 1) * HGRN_DK)
            self.scores.append(_dot_nt(qa[:, hs], kb[:, hs]))
            self.kv.append([_dot_tn(iv[ci * c_len:(ci + 1) * c_len, hs], ke[ci * c_len:(ci + 1) * c_len, hs])
                            for ci in range(nc)])

    def second_matmuls(self):
        nc, c_len = self.nc, GLA_CHUNK
        order = range(nc - 1, -1, -1) if self.rev else range(nc)
        for h in range(HGRN_HEADS):
            hs = slice(h * HGRN_DK, (h + 1) * HGRN_DK)
            o_intra = _dot(jnp.where(self.keep, self.scores[h], 0.0), self.iv[:, hs])
            st = self.st_ref[h]
            for ci in order:
                rows = slice(ci * c_len, (ci + 1) * c_len)
                self.o_ref[0, rows, hs] = o_intra[rows] + _dot_nt(self.qe[rows, hs], st)
                st = st * self.dec[ci, :, hs] + self.kv[h][ci]
            self.st_ref[h] = st


def _hgrn_kernel(qf_ref, ff_ref, if_ref, qb_ref, fb_ref, ib_ref, s0f_ref, s0b_ref,
                 of_ref, ob_ref, sff_ref, sfb_ref, st_ref, *, tb):
    j = pl.program_id(1)

    @pl.when(j == 0)
    def _():
        st_ref[0] = s0f_ref[0]
        st_ref[1] = s0b_ref[0]

    fw = _HgrnDirection(qf_ref, ff_ref, if_ref, st_ref.at[0], of_ref, rev=False, tb=tb)
    bw = _HgrnDirection(qb_ref, fb_ref, ib_ref, st_ref.at[1], ob_ref, rev=True, tb=tb)
    fw.gates()
    bw.gates()
    fw.first_matmuls()
    bw.first_matmuls()
    fw.second_matmuls()
    bw.second_matmuls()

    @pl.when(j == pl.num_programs(1) - 1)
    def _():
        sff_ref[0] = st_ref[0]
        sfb_ref[0] = st_ref[1]


def _hgrn_scan(hq, hl, hi, s0_fw, s0_bw, *, tb):
    b, l, _ = hq.shape
    nb = l // tb
    fcol = lambda cidx: pl.BlockSpec((1, tb, HGRN_WIDTH), lambda i, j: (i, j, cidx))
    bcol = lambda cidx: pl.BlockSpec((1, tb, HGRN_WIDTH), lambda i, j: (i, nb - 1 - j, cidx))
    st_shape = (HGRN_HEADS, HGRN_DK, HGRN_DK)
    st_spec = pl.BlockSpec((1,) + st_shape, lambda i, j: (i, 0, 0, 0))
    o_shape = jax.ShapeDtypeStruct((b, l, HGRN_WIDTH), F32)
    s_shape = jax.ShapeDtypeStruct((b,) + st_shape, F32)
    return pl.pallas_call(
        functools.partial(_hgrn_kernel, tb=tb),
        out_shape=(o_shape, o_shape, s_shape, s_shape),
        grid=(b, nb),
        in_specs=[fcol(0), fcol(0), fcol(0), bcol(0), bcol(1), bcol(0), st_spec, st_spec],
        out_specs=(fcol(0), bcol(0), st_spec, st_spec),
        scratch_shapes=[pltpu.VMEM((2,) + st_shape, F32)],
        compiler_params=_cparams(("parallel", "arbitrary")),
    )(hq, hl, hi, hq, hl, hi, s0_fw, s0_bw)


def _swiglu_residual(x1, mod_ref, ng2_ref, wi_ref, wd_ref):
    h2 = _mx(_norm_mod(x1, ng2_ref[...], mod_ref[3:4, :], mod_ref[4:5, :]))
    acc = None
    for lo in range(0, FFN_HIDDEN, FFN_CHUNK):
        gate = jnp.dot(h2, wi_ref[:, lo:lo + FFN_CHUNK], preferred_element_type=F32)
        up = jnp.dot(h2, wi_ref[:, FFN_HIDDEN + lo:FFN_HIDDEN + lo + FFN_CHUNK], preferred_element_type=F32)
        part = _dot(_silu(gate) * up, wd_ref[lo:lo + FFN_CHUNK, :])
        acc = part if acc is None else acc + part
    return x1 + mod_ref[5:6, :] * acc


def _out_kernel(*refs, n_mix, gated, n_split):
    mix_refs = refs[:n_mix]
    rest = refs[n_mix:]
    if gated:
        of_ref, ob_ref, g_ref, og_ref = rest[:4]
        rest = rest[4:]
    x_ref, mod_ref, ng2_ref, wo_ref, wi_ref, wd_ref, o_ref = rest
    size = x_ref.shape[1] // n_split

    def attn_residual(rows):
        mixes = [m_ref[0, rows, :] for m_ref in mix_refs]
        if gated:
            o = of_ref[0, rows, :] + ob_ref[0, rows, :]
            og = og_ref[...]
            parts = []
            for h in range(HGRN_HEADS):
                oh = o[:, h * HGRN_DK:(h + 1) * HGRN_DK]
                parts.append(oh * lax.rsqrt(_mean_rows(oh * oh) + EPS) * og)
            mixes.append(_mx(jnp.concatenate(parts, axis=1) * g_ref[0, rows, :].astype(F32)))
        y, off = None, 0
        for m in mixes:
            w = m.shape[-1]
            part = jnp.dot(m, wo_ref[off:off + w, :], preferred_element_type=F32)
            y = part if y is None else y + part
            off += w
        return x_ref[0, rows, :] + mod_ref[2:3, :] * y

    groups = [slice(k * size, (k + 1) * size) for k in range(n_split)]
    x1 = [attn_residual(rows) for rows in groups]
    for rows, x1_k in zip(groups, x1):
        o_ref[0, rows, :] = _swiglu_residual(x1_k, mod_ref, ng2_ref, wi_ref, wd_ref)


def _out_layer(mixes, x, mod4, layer, row_of_batch, ng2, wo, wi, wd, *, tm, gated=None, n_split=1):
    b, l, d = x.shape
    row = lambda w: pl.BlockSpec((1, tm, w), lambda i, j: (i, j, 0))
    in_specs = [row(m.shape[-1]) for m in mixes]
    args = list(mixes)
    if gated is not None:
        in_specs += [row(HGRN_WIDTH), row(HGRN_WIDTH), row(HGRN_WIDTH), _const_spec((1, HGRN_DK))]
        args += list(gated)
    in_specs += [row(d),
                 pl.BlockSpec((None, None, 6, d), lambda i, j: (layer, row_of_batch(i), 0, 0)),
                 _const_spec((1, d)), _const_spec(wo.shape), _const_spec(wi.shape), _const_spec(wd.shape)]
    return pl.pallas_call(
        functools.partial(_out_kernel, n_mix=len(mixes), gated=gated is not None, n_split=n_split),
        out_shape=jax.ShapeDtypeStruct((b, l, d), F32),
        grid=(b, l // tm),
        in_specs=in_specs,
        out_specs=row(d),
        compiler_params=_cparams(("parallel", "parallel")),
    )(*args, x, mod4, ng2, wo, wi, wd)


def _rope_ret(u, cs, sn):
    parts = []
    for h in range(RET_HEADS):
        x1 = u[:, h * RET_DK:h * RET_DK + LANES]
        x2 = u[:, h * RET_DK + LANES:(h + 1) * RET_DK]
        parts += [x1 * cs - x2 * sn, x2 * cs + x1 * sn]
    return jnp.concatenate(parts, axis=1)


def _ret_log_gamma(h, rev):
    hh = RET_HEADS - 1 - h if rev else h
    return math.log(1.0 - 2.0 ** (-5.0 - hh))


def _ret_weights(n, lg, rev):
    pos = lax.broadcasted_iota(jnp.int32, (n, 1), 0).astype(F32)
    if rev:
        return jnp.exp(lg * pos), jnp.exp(lg * (n - pos))
    return jnp.exp(lg * (n - 1.0 - pos)), jnp.exp(lg * (pos + 1.0))


def _ret_tables(n, rev):
    pos = np.arange(n, dtype=np.float64)[:, None]
    out = np.zeros((RET_HEADS, 2, n, LANES))
    for h in range(RET_HEADS):
        lg = _ret_log_gamma(h, rev)
        out[h, 0] = np.exp(lg * pos) if rev else np.exp(lg * (n - 1.0 - pos))
        out[h, 1] = np.exp(lg * (n - pos)) if rev else np.exp(lg * (pos + 1.0))
    return jnp.asarray(out, F32)


def _ret_init_state(kc_ref, vc_ref, s_ref, rev):
    lc = kc_ref.shape[1]
    for h in range(RET_HEADS):
        kw, _ = _ret_weights(lc, _ret_log_gamma(h, rev), rev)
        kc = kc_ref[0, :, h * RET_DK:(h + 1) * RET_DK].astype(F32) * kw
        s_ref[h] = _dot_tn(kc, vc_ref[0, :, h * RET_DV:(h + 1) * RET_DV])


def _proj_odd_kernel(x_ref, mod_ref, ng_ref, w_ref, cs_ref, sn_ref, kc_ref, vc_ref, wt_ref,
                     q_ref, k_ref, v_ref, g_ref, ob_ref, s_ref, *, chunk):
    @pl.when(pl.program_id(1) == 0)
    def _():
        _ret_init_state(kc_ref, vc_ref, s_ref, True)

    h = _mx(_norm_mod(x_ref[0], ng_ref[...], mod_ref[0:1, :], mod_ref[1:2, :]))
    cs, sn = cs_ref[...], sn_ref[...]
    g = jnp.dot(h, w_ref[:, 2 * ODD_Q + ODD_V:], preferred_element_type=F32)
    g_ref[0] = _silu(g).astype(g_ref.dtype)
    q = _mx(_rope_ret(jnp.dot(h, w_ref[:, :ODD_Q], preferred_element_type=F32), cs, sn))
    q_ref[0] = q
    k = _rope_ret(jnp.dot(h, w_ref[:, ODD_Q:2 * ODD_Q], preferred_element_type=F32), cs, sn) * RET_DK ** -0.5
    k_ref[0] = k.astype(k_ref.dtype)
    v = _mx(jnp.dot(h, w_ref[:, 2 * ODD_Q:2 * ODD_Q + ODD_V], preferred_element_type=F32))
    v_ref[0] = v
    for ci in range(x_ref.shape[1] // chunk - 1, -1, -1):
        rows = slice(ci * chunk, (ci + 1) * chunk)
        for hd in range(RET_HEADS):
            ks, vs = slice(hd * RET_DK, (hd + 1) * RET_DK), slice(hd * RET_DV, (hd + 1) * RET_DV)
            s = s_ref[hd]
            ob_ref[0, rows, vs] = _scale_rows(_dot(q[rows, ks], s), wt_ref[hd, 1]).astype(ob_ref.dtype)
            s_ref[hd] = (s * math.exp(_ret_log_gamma(hd, True) * chunk)
                         + _dot_tn(_scale_rows(k[rows, ks], wt_ref[hd, 0]), v[rows, vs]))


def _proj_odd(x, mod4, layer, ng, w, cs, sn, kc, vc, *, tm, chunk):
    b, l, d = x.shape
    lc = kc.shape[1]
    nb = l // tm
    row = lambda wd: pl.BlockSpec((1, tm, wd), lambda i, j: (i, nb - 1 - j, 0))
    tab = pl.BlockSpec((tm, LANES), lambda i, j: (nb - 1 - j, 0))
    ctx = lambda wd: pl.BlockSpec((1, lc, wd), lambda i, j: (i, 0, 0))
    wide = jax.ShapeDtypeStruct((b, l, ODD_V), MXU_DTYPE)
    narrow = jax.ShapeDtypeStruct((b, l, ODD_Q), MXU_DTYPE)
    return pl.pallas_call(
        functools.partial(_proj_odd_kernel, chunk=chunk),
        out_shape=(narrow, narrow, wide, wide, wide),
        grid=(b, nb),
        in_specs=[row(d),
                  pl.BlockSpec((None, None, 6, d), lambda i, j: (layer, i, 0, 0)),
                  _const_spec((1, d)), _const_spec((d, ODD_IN)), tab, tab, ctx(ODD_Q), ctx(ODD_V),
                  _const_spec((RET_HEADS, 2, chunk, LANES))],
        out_specs=(row(ODD_Q), row(ODD_Q), row(ODD_V), row(ODD_V), row(ODD_V)),
        scratch_shapes=[pltpu.VMEM((RET_HEADS, RET_DK, RET_DV), F32)],
        compiler_params=_cparams(("parallel", "arbitrary")),
    )(x, mod4, ng, w, cs, sn, kc, vc, _ret_tables(chunk, True))


def _proj_ctx_kv_kernel(x_ref, mod_ref, ng_ref, w_ref, k_ref, v_ref):
    h = _mx(_norm_mod(x_ref[0], ng_ref[...], mod_ref[0:1, :], mod_ref[1:2, :]))
    k = jnp.dot(h, w_ref[:, :ODD_Q], preferred_element_type=F32)
    k_ref[0] = (k * RET_DK ** -0.5).astype(k_ref.dtype)
    v_ref[0] = jnp.dot(h, w_ref[:, ODD_Q:], preferred_element_type=F32).astype(v_ref.dtype)


def _proj_ctx_kv(ctx, mod4, layer, ctx_row, ng, w_kv):
    b, lc, d = ctx.shape
    return pl.pallas_call(
        _proj_ctx_kv_kernel,
        out_shape=(jax.ShapeDtypeStruct((b, lc, ODD_Q), MXU_DTYPE),
                   jax.ShapeDtypeStruct((b, lc, ODD_V), MXU_DTYPE)),
        grid=(b,),
        in_specs=[pl.BlockSpec((1, lc, d), lambda i: (i, 0, 0)),
                  pl.BlockSpec((None, None, 6, d), lambda i: (layer, ctx_row, 0, 0)),
                  _const_spec((1, d)), _const_spec((d, ODD_Q + ODD_V))],
        out_specs=(pl.BlockSpec((1, lc, ODD_Q), lambda i: (i, 0, 0)),
                   pl.BlockSpec((1, lc, ODD_V), lambda i: (i, 0, 0))),
        compiler_params=_cparams(("parallel",)),
    )(ctx, mod4, ng, w_kv)


def _ret_kernel(q_ref, k_ref, v_ref, kc_ref, vc_ref, dm_ref, wt_ref, ob_ref, sg_ref, o_ref, s_ref, *, chunk):
    @pl.when(pl.program_id(1) == 0)
    def _():
        _ret_init_state(kc_ref, vc_ref, s_ref, False)

    state = [s_ref[h] for h in range(RET_HEADS)]
    staged = []
    for ci in range(q_ref.shape[1] // chunk):
        rows = slice(ci * chunk, (ci + 1) * chunk)
        for h in range(RET_HEADS):
            q = q_ref[0, rows, h * RET_DK:(h + 1) * RET_DK]
            k = k_ref[0, rows, h * RET_DK:(h + 1) * RET_DK]
            v = v_ref[0, rows, h * RET_DV:(h + 1) * RET_DV]
            o = _scale_rows(_dot(q, state[h]), wt_ref[h, 1])
            sc = _dot_nt(q, k)
            state[h] = (state[h] * math.exp(_ret_log_gamma(h, False) * chunk)
                        + _dot_tn(_scale_rows(k.astype(F32), wt_ref[h, 0]), v))
            staged.append((rows, h, o, sc, v))
    for h in range(RET_HEADS):
        s_ref[h] = state[h]
    for rows, h, o, sc, v in staged:
        vs = slice(h * RET_DV, (h + 1) * RET_DV)
        o = o + _dot(sc * dm_ref[h], v) + ob_ref[0, rows, vs].astype(F32)
        o = o * lax.rsqrt(_mean_rows(o * o) + EPS)
        o_ref[0, rows, vs] = (o * sg_ref[0, rows, vs].astype(F32)).astype(o_ref.dtype)


def _ret_decay_matrix(tb):
    rel = np.arange(tb)[:, None] - np.arange(tb)[None, :]
    out = np.zeros((RET_HEADS, tb, tb), np.float64)
    for h in range(RET_HEADS):
        fw = np.exp(_ret_log_gamma(h, False) * np.maximum(rel, 0))
        bw = np.exp(_ret_log_gamma(h, True) * np.maximum(-rel, 0))
        out[h] = np.where(rel > 0, fw, np.where(rel < 0, bw, 2.0))
    return jnp.asarray(out, F32)


def _ret_scan(q, k, v, kc, vc, o_bw, sg, *, tb, chunk):
    b, l, _ = q.shape
    lc = kc.shape[1]
    row = lambda w: pl.BlockSpec((1, tb, w), lambda i, j: (i, j, 0))
    ctx = lambda w: pl.BlockSpec((1, lc, w), lambda i, j: (i, 0, 0))
    return pl.pallas_call(
        functools.partial(_ret_kernel, chunk=chunk),
        out_shape=jax.ShapeDtypeStruct((b, l, ODD_V), MXU_DTYPE),
        grid=(b, l // tb),
        in_specs=[row(ODD_Q), row(ODD_Q), row(ODD_V), ctx(ODD_Q), ctx(ODD_V),
                  _const_spec((RET_HEADS, chunk, chunk)), _const_spec((RET_HEADS, 2, chunk, LANES)),
                  row(ODD_V), row(ODD_V)],
        out_specs=row(ODD_V),
        scratch_shapes=[pltpu.VMEM((RET_HEADS, RET_DK, RET_DV), F32)],
        compiler_params=_cparams(("parallel", "arbitrary")),
    )(q, k, v, kc, vc, _ret_decay_matrix(chunk), _ret_tables(chunk, False), o_bw, sg)


def _axial_tables(n_tok):
    t = np.arange(n_tok)
    n_freq = HEAD_DIM // 4
    inv = ROPE_BASE ** (-np.arange(n_freq, dtype=np.float64) / n_freq)
    ang = np.concatenate([(t // GRID_W)[:, None] * inv, (t % GRID_W)[:, None] * inv], axis=-1)
    cos, sin = np.cos(ang), np.sin(ang)
    cs = np.tile(np.concatenate([cos, cos], axis=-1), (1, LANES // HEAD_DIM))
    sn = np.tile(np.concatenate([-sin, sin], axis=-1), (1, LANES // HEAD_DIM))
    return jnp.asarray(cs, F32), jnp.asarray(sn, F32)


def _retention_tables(n_tok):
    theta = 1.0 / (RET_BASE ** np.linspace(0.0, 1.0, RET_DK // 2))
    ang = np.arange(n_tok, dtype=np.float64)[:, None] * theta
    return jnp.asarray(np.cos(ang), F32), jnp.asarray(np.sin(ang), F32)


def _block_diag_mean(width, head):
    idx = np.arange(width) // head
    return jnp.asarray((idx[:, None] == idx[None, :]) / head, MXU_DTYPE)


def _pick(n, pref):
    while n % pref:
        pref //= 2
    return pref


def kernel(x, c, ctx, c_ctx, mod_w, mod_b, norm_g, ffn_w_in, ffn_w_out, even_w_in, even_w_out,
           attn_qk_norm_g, attn_sink, hgrn_out_norm_g, hgrn_lb, odd_w_in, odd_w_out):
    n_b, n_tok, d = x.shape
    n_ctx = ctx.shape[1]
    depth = mod_w.shape[0]
    assert depth == 2 and d == D_MODEL

    rows = -(-(n_b + 1) // 8) * 8
    cond = jnp.zeros((rows, d), F32).at[:n_b].set(c).at[n_b].set(c_ctx)
    mod4 = _modulation(cond, mod_w, mod_b).reshape(depth, rows, 6, d)
    lat_row = lambda i: i
    ctx_row = lambda i: n_b

    ng1, ng2 = norm_g[0, 0][None, :], norm_g[0, 1][None, :]
    w_in = _mx(even_w_in[0])
    gq = jnp.tile(attn_qk_norm_g[0, 0], ATTN_HEADS)[None, :]
    gk = jnp.tile(attn_qk_norm_g[0, 1], ATTN_KV_HEADS)[None, :]
    bdq, bdk = _block_diag_mean(A_Q, HEAD_DIM), _block_diag_mean(A_KV, HEAD_DIM)
    cs_a, sn_a = _axial_tables(n_tok)
    lower = jnp.cumsum(jax.nn.softmax(hgrn_lb.astype(F32), axis=0), axis=0)[0][None, :]
    og = hgrn_out_norm_g[0][None, :]
    wo = _mx(even_w_out[0])
    wi, wd = _mx(ffn_w_in[0]), _mx(ffn_w_out[0])

    tm_c = _pick(n_ctx, 256)
    tm_l = _pick(n_tok, 512)
    ones_c = jnp.ones((n_ctx, LANES), F32)
    qc, kc, vc, hqc, hlc, hic, hgc = _proj_even(ctx, mod4, 0, ctx_row, ng1, w_in, gq, gk, bdq, bdk, ones_c, ones_c,
                                                lower, rope=False, tm=tm_c)
    ql, kl, vl, hql, hll, hil, hgl = _proj_even(x, mod4, 0, lat_row, ng1, w_in, gq, gk, bdq, bdk, cs_a, sn_a,
                                                lower, rope=True, tm=_pick(n_tok, 1024), n_split=2)

    a_ctx = _attention(qc, None, None, kc, vc, attn_sink[0], window=False, tq=WINDOW)
    a_lat = _attention(ql, kl, vl, kc, vc, attn_sink[0], window=True, tq=_pick(n_tok, 4 * WINDOW))

    zeros = jnp.zeros((n_b, HGRN_HEADS, HGRN_DK, HGRN_DK), F32)
    tb_c, tb_l = _pick(n_ctx, 256), _pick(n_tok, 256)
    o_fw_c, o_bw_c, s_fw, s_bw = _hgrn_scan(hqc, hlc, hic, zeros, zeros, tb=tb_c)
    o_fw, o_bw, _, _ = _hgrn_scan(hql, hll, hil, s_fw, s_bw, tb=tb_l)

    ctx1 = _out_layer([a_ctx], ctx, mod4, 0, ctx_row, ng2, wo, wi, wd, tm=tm_c,
                      gated=(o_fw_c, o_bw_c, hgc, og))
    x1 = _out_layer([a_lat], x, mod4, 0, lat_row, ng2, wo, wi, wd, tm=_pick(n_tok, 1024), n_split=2,
                    gated=(o_fw, o_bw, hgl, og))

    ng1, ng2 = norm_g[1, 0][None, :], norm_g[1, 1][None, :]
    w_in = _mx(odd_w_in[0])
    cs_r, sn_r = _retention_tables(n_tok)
    wo = _mx(odd_w_out[0])
    wi, wd = _mx(ffn_w_in[1]), _mx(ffn_w_out[1])

    kc, vc = _proj_ctx_kv(ctx1, mod4, 1, n_b, ng1, w_in[:, ODD_Q:2 * ODD_Q + ODD_V])
    tb_r = _pick(n_tok, 256)
    q, k, v, sg, o_bw = _proj_odd(x1, mod4, 1, ng1, w_in, cs_r, sn_r, kc, vc, tm=_pick(n_tok, 512), chunk=tb_r)
    mix = _ret_scan(q, k, v, kc, vc, o_bw, sg, tb=_pick(n_tok, 512), chunk=tb_r)
    return _out_layer([mix], x1, mod4, 1, lat_row, ng2, wo, wi, wd, tm=_pick(n_tok, 1024), n_split=2)
```

```python
import functools
import math

import numpy as np
import jax
import jax.numpy as jnp
from jax import lax
from jax.experimental import pallas as pl
from jax.experimental.pallas import tpu as pltpu

F32 = jnp.float32
MXU_DTYPE = jnp.bfloat16

D_MODEL = 1024
GRID_W = 64
HEAD_DIM = 64
ATTN_HEADS = 8
ATTN_KV_HEADS = 2
WINDOW = 128
ROPE_BASE = 10000.0
HGRN_HEADS = 4
HGRN_DK = 128
HGRN_WIDTH = HGRN_HEADS * HGRN_DK
GLA_CHUNK = 64
RET_HEADS = 4
RET_DK = 256
RET_DV = 512
RET_BASE = 10000.0
FFN_HIDDEN = 2816
A_Q = ATTN_HEADS * HEAD_DIM
A_KV = ATTN_KV_HEADS * HEAD_DIM
EVEN_IN = A_Q + 2 * A_KV + 5 * HGRN_WIDTH
HG_OFF = A_Q + 2 * A_KV
ODD_Q = RET_HEADS * RET_DK
ODD_V = RET_HEADS * RET_DV
ODD_IN = 2 * ODD_Q + 2 * ODD_V
EPS = 1e-6
NEG = -1e30
LOG2E = 1.4426950408889634

LANES = 128
MXU_TILE = 256
FFN_CHUNK = MXU_TILE
VMEM_LIMIT = 56 * 1024 * 1024


def _mx(a):
    return a.astype(MXU_DTYPE)


def _dot(a, b):
    return jnp.dot(_mx(a), _mx(b), preferred_element_type=F32)


def _dot_nt(a, b):
    return lax.dot_general(_mx(a), _mx(b), (((1,), (1,)), ((), ())), preferred_element_type=F32)


def _dot_tn(a, b):
    return lax.dot_general(_mx(a), _mx(b), (((0,), (0,)), ((), ())), preferred_element_type=F32)


def _split(x):
    hi = _mx(x)
    lo = _mx(x - hi.astype(F32))
    return hi, lo


def _sigmoid(x):
    return 1.0 / (1.0 + jnp.exp(-x))


def _silu(x):
    return x * _sigmoid(x)


def _norm_mod(x, g, shift, scale):
    y = x * lax.rsqrt(_mean_rows(x * x) + EPS) * g
    return y * (1.0 + scale) + shift


def _lane_tiles(x):
    return [x[:, t * LANES:(t + 1) * LANES] for t in range(x.shape[1] // LANES)]


def _scale_rows(x, tab):
    return jnp.concatenate([t * tab for t in _lane_tiles(x)], axis=1)


def _mean_rows(x):
    return jnp.sum(functools.reduce(jnp.add, _lane_tiles(x)), axis=-1, keepdims=True) * (1.0 / x.shape[1])


def _cparams(sem):
    return pltpu.CompilerParams(dimension_semantics=sem, vmem_limit_bytes=VMEM_LIMIT)


def _const_spec(shape):
    nd = len(shape)
    return pl.BlockSpec(shape, lambda *_: (0,) * nd, pipeline_mode=pl.Buffered(1))


def _mod_kernel(c_ref, w_ref, b_ref, o_ref):
    s = _silu(c_ref[...])
    o_ref[0] = jnp.dot(s, w_ref[0], preferred_element_type=F32,
                       precision=lax.Precision.HIGHEST) + b_ref[0]


def _modulation(cond, mod_w, mod_b):
    depth, d, n = mod_w.shape
    rows = cond.shape[0]
    tn = 1536
    return pl.pallas_call(
        _mod_kernel,
        out_shape=jax.ShapeDtypeStruct((depth, rows, n), F32),
        grid=(depth, n // tn),
        in_specs=[pl.BlockSpec((rows, d), lambda l, j: (0, 0)),
                  pl.BlockSpec((1, d, tn), lambda l, j: (l, 0, j)),
                  pl.BlockSpec((1, 1, tn), lambda l, j: (l, 0, j))],
        out_specs=pl.BlockSpec((1, rows, tn), lambda l, j: (l, 0, j)),
        compiler_params=_cparams(("arbitrary", "arbitrary")),
    )(cond, mod_w, mod_b.reshape(depth, 1, n))


def _head_rms(u, bd, gain):
    hi, lo = _split(u * u)
    width = u.shape[1]
    step = min(width, MXU_TILE)
    ms = jnp.concatenate(
        [jnp.dot(hi[:, c:c + step], bd[c:c + step, c:c + step], preferred_element_type=F32)
         + jnp.dot(lo[:, c:c + step], bd[c:c + step, c:c + step], preferred_element_type=F32)
         for c in range(0, width, step)], axis=1)
    return u * lax.rsqrt(ms + EPS) * gain


def _rope_half_swap(u):
    lane = lax.broadcasted_iota(jnp.int32, u.shape, 1)
    first = (lane % HEAD_DIM) < (HEAD_DIM // 2)
    return jnp.where(first, pltpu.roll(u, LANES - HEAD_DIM // 2, 1), pltpu.roll(u, HEAD_DIM // 2, 1))


def _dup_halves(u):
    lane = lax.broadcasted_iota(jnp.int32, u.shape, 1)
    low = lane < HEAD_DIM
    swapped = pltpu.roll(u, HEAD_DIM, 1)
    return jnp.concatenate([jnp.where(low, u, swapped), jnp.where(low, swapped, u)], axis=1)


def _proj_even_kernel(x_ref, mod_ref, ng_ref, w_ref, gq_ref, gk_ref, bdq_ref, bdk_ref, cs_ref, sn_ref, lb_ref,
                      q_ref, k_ref, v_ref, hq_ref, hl_ref, hi_ref, hg_ref, *, rope, n_split):
    lb = lb_ref[...]
    col = lambda c: w_ref[:, HG_OFF + c * HGRN_WIDTH:HG_OFF + (c + 1) * HGRN_WIDTH]
    log_gate = lambda raw: jnp.log(lb + (1.0 - lb) * _sigmoid(raw))
    size = x_ref.shape[1] // n_split
    for grp in range(n_split):
        rows = slice(grp * size, (grp + 1) * size)
        h = _mx(_norm_mod(x_ref[0, rows, :], ng_ref[...], mod_ref[0:1, :], mod_ref[1:2, :]))
        hl_ref[0, rows, :HGRN_WIDTH] = log_gate(jnp.dot(h, col(1), preferred_element_type=F32))
        hl_ref[0, rows, HGRN_WIDTH:] = log_gate(jnp.dot(h, col(2), preferred_element_type=F32))
        qkv = jnp.dot(h, w_ref[:, :HG_OFF], preferred_element_type=F32)
        hg_ref[0, rows, :] = _silu(jnp.dot(h, col(4), preferred_element_type=F32)).astype(hg_ref.dtype)
        q = _head_rms(qkv[:, :A_Q], bdq_ref[...], gq_ref[...])
        k = _head_rms(qkv[:, A_Q:A_Q + A_KV], bdk_ref[...], gk_ref[...])
        hq_ref[0, rows, :] = _silu(jnp.dot(h, col(0), preferred_element_type=F32))
        hi_ref[0, rows, :] = jnp.dot(h, col(3), preferred_element_type=F32).astype(hi_ref.dtype)
        v = qkv[:, A_Q + A_KV:HG_OFF]
        if rope:
            cs, sn = cs_ref[rows, :], sn_ref[rows, :]
            q = jnp.concatenate(
                [q[:, c * LANES:(c + 1) * LANES] * cs + _rope_half_swap(q[:, c * LANES:(c + 1) * LANES]) * sn
                 for c in range(A_Q // LANES)], axis=1)
            k = k * cs + _rope_half_swap(k) * sn
        q_ref[0, rows, :] = (q * (HEAD_DIM ** -0.5 * LOG2E)).astype(q_ref.dtype)
        k_ref[0, rows, :] = _dup_halves(k).astype(k_ref.dtype)
        v_ref[0, rows, :] = _dup_halves(v).astype(v_ref.dtype)


def _proj_even(x, mod4, layer, row_of_batch, ng, w, gq, gk, bdq, bdk, cs, sn, lb, *, rope, tm, n_split=1):
    b, l, d = x.shape
    nblk = l // tm
    kern = functools.partial(_proj_even_kernel, rope=rope, n_split=n_split)
    tab_spec = pl.BlockSpec((tm, LANES), lambda i, j: (j, 0))
    return pl.pallas_call(
        kern,
        out_shape=(jax.ShapeDtypeStruct((b, l, A_Q), MXU_DTYPE),
                   jax.ShapeDtypeStruct((b, l, 2 * A_KV), MXU_DTYPE),
                   jax.ShapeDtypeStruct((b, l, 2 * A_KV), MXU_DTYPE),
                   jax.ShapeDtypeStruct((b, l, HGRN_WIDTH), F32),
                   jax.ShapeDtypeStruct((b, l, 2 * HGRN_WIDTH), F32),
                   jax.ShapeDtypeStruct((b, l, HGRN_WIDTH), MXU_DTYPE),
                   jax.ShapeDtypeStruct((b, l, HGRN_WIDTH), MXU_DTYPE)),
        grid=(b, nblk),
        in_specs=[pl.BlockSpec((1, tm, d), lambda i, j: (i, j, 0)),
                  pl.BlockSpec((None, None, 6, d), lambda i, j: (layer, row_of_batch(i), 0, 0)),
                  _const_spec((1, d)),
                  _const_spec((d, EVEN_IN)),
                  _const_spec((1, A_Q)), _const_spec((1, A_KV)),
                  _const_spec((A_Q, A_Q)), _const_spec((A_KV, A_KV)),
                  tab_spec, tab_spec, _const_spec((1, HGRN_WIDTH))],
        out_specs=(pl.BlockSpec((1, tm, A_Q), lambda i, j: (i, j, 0)),
                   pl.BlockSpec((1, tm, 2 * A_KV), lambda i, j: (i, j, 0)),
                   pl.BlockSpec((1, tm, 2 * A_KV), lambda i, j: (i, j, 0)),
                   pl.BlockSpec((1, tm, HGRN_WIDTH), lambda i, j: (i, j, 0)),
                   pl.BlockSpec((1, tm, 2 * HGRN_WIDTH), lambda i, j: (i, j, 0)),
                   pl.BlockSpec((1, tm, HGRN_WIDTH), lambda i, j: (i, j, 0)),
                   pl.BlockSpec((1, tm, HGRN_WIDTH), lambda i, j: (i, j, 0))),
        compiler_params=_cparams(("parallel", "parallel")),
    )(x, mod4, ng, w, gq, gk, bdq, bdk, cs, sn, lb)


def _attn_kernel(*refs, window, tq, seq):
    if window:
        (q_ref, kp_ref, kc_ref, kn_ref, vp_ref, vc_ref, vn_ref, kx_ref, vx_ref, sink_ref, o_ref) = refs
    else:
        (q_ref, kx_ref, vx_ref, sink_ref, o_ref) = refs
    i = pl.program_id(1)
    n_ctx = kx_ref.shape[1]
    sub = WINDOW
    span = 3 * sub
    lane = lax.broadcasted_iota(jnp.int32, (2 * sub, LANES), 1)
    low = lane < HEAD_DIM
    if window:
        rr = lax.broadcasted_iota(jnp.int32, (sub, sub), 0)
        cc = lax.broadcasted_iota(jnp.int32, (sub, sub), 1)
    ones = jnp.ones((n_ctx + (span if window else 0), LANES), MXU_DTYPE)

    def scores(sb, g):
        rows = slice(sb * sub, (sb + 1) * sub)
        cols = slice(g * LANES, (g + 1) * LANES)
        if window:
            kfull = jnp.concatenate([kp_ref[0, :, cols], kc_ref[0, :, cols], kn_ref[0, :, cols]], axis=0)
            kk = jnp.concatenate([kfull[sb * sub:sb * sub + span], kx_ref[0, :, cols]], axis=0)
        else:
            kk = kx_ref[0, :, cols]
        q2 = jnp.concatenate([q_ref[0, rows, (2 * g) * LANES:(2 * g + 1) * LANES],
                              q_ref[0, rows, (2 * g + 1) * LANES:(2 * g + 2) * LANES]], axis=0)
        zero = jnp.zeros_like(q2)
        q4 = jnp.concatenate([jnp.where(low, q2, zero), jnp.where(low, zero, q2)], axis=0)
        return _dot_nt(q4, kk)

    def finish(sb, g, s):
        rows = slice(sb * sub, (sb + 1) * sub)
        cols = slice(g * LANES, (g + 1) * LANES)
        tiles = [s[:, t * LANES:(t + 1) * LANES] for t in range(s.shape[1] // LANES)]
        if window:
            base = i * tq + sb * sub
            ok_prev = (cc >= rr) & (base - sub + cc >= 0)
            ok_next = (cc <= rr) & (base + sub + cc < seq)
            tiles[0] = tiles[0] + jnp.concatenate([jnp.where(ok_prev, 0.0, NEG)] * 4, axis=0)
            tiles[2] = tiles[2] + jnp.concatenate([jnp.where(ok_next, 0.0, NEG)] * 4, axis=0)
            s = jnp.concatenate(tiles, axis=1)
            vfull = jnp.concatenate([vp_ref[0, :, cols], vc_ref[0, :, cols], vn_ref[0, :, cols]], axis=0)
            vv = jnp.concatenate([vfull[sb * sub:sb * sub + span], vx_ref[0, :, cols]], axis=0)
        else:
            vv = vx_ref[0, :, cols]
        tmax = functools.reduce(jnp.maximum, tiles)
        sink = sink_ref[g]
        m = jnp.maximum(jnp.max(tmax, axis=-1, keepdims=True), sink)
        p = jnp.exp2(s - m)
        pv = _dot(p, jnp.concatenate([vv, ones], axis=1))
        o = pv[:, :LANES] * (1.0 / (pv[:, LANES:] + jnp.exp2(sink - m)))
        o2 = jnp.where(low, o[:2 * sub], o[2 * sub:])
        o_ref[0, rows, (2 * g) * LANES:(2 * g + 1) * LANES] = o2[:sub].astype(o_ref.dtype)
        o_ref[0, rows, (2 * g + 1) * LANES:(2 * g + 2) * LANES] = o2[sub:].astype(o_ref.dtype)

    units = [(sb, g) for sb in range(tq // sub) for g in range(ATTN_KV_HEADS)]
    ahead = 1
    pending = [scores(*u) for u in units[:ahead]]
    for idx, unit in enumerate(units):
        if idx + ahead < len(units):
            pending.append(scores(*units[idx + ahead]))
        finish(*unit, pending.pop(0))


def _sink_rows(sink, tq):
    s = sink.astype(F32).reshape(ATTN_KV_HEADS, 4) * LOG2E
    order = jnp.array([0, 2, 1, 3])
    return jnp.repeat(s[:, order], tq, axis=1)[:, :, None]


def _attention(q, k, v, kx, vx, sink, *, window, tq):
    b, l, _ = q.shape
    n_ctx = kx.shape[1]
    nq = l // tq
    kern = functools.partial(_attn_kernel, window=window, tq=tq, seq=l)
    qspec = pl.BlockSpec((1, tq, A_Q), lambda i, j: (i, j, 0))
    xspec = pl.BlockSpec((1, n_ctx, 2 * A_KV), lambda i, j: (i, 0, 0))
    sspec = _const_spec((ATTN_KV_HEADS, 4 * WINDOW, 1))
    if window:
        r = tq // WINDOW
        prev = pl.BlockSpec((1, WINDOW, 2 * A_KV), lambda i, j: (i, jnp.maximum(j * r - 1, 0), 0))
        cur = pl.BlockSpec((1, tq, 2 * A_KV), lambda i, j: (i, j, 0))
        nxt = pl.BlockSpec((1, WINDOW, 2 * A_KV), lambda i, j: (i, jnp.minimum((j + 1) * r, nq * r - 1), 0))
        in_specs = [qspec, prev, cur, nxt, prev, cur, nxt, xspec, xspec, sspec]
        args = (q, k, k, k, v, v, v, kx, vx, _sink_rows(sink, WINDOW))
    else:
        in_specs = [qspec, xspec, xspec, sspec]
        args = (q, kx, vx, _sink_rows(sink, WINDOW))
    return pl.pallas_call(
        kern,
        out_shape=jax.ShapeDtypeStruct((b, l, A_Q), MXU_DTYPE),
        grid=(b, nq),
        in_specs=in_specs,
        out_specs=pl.BlockSpec((1, tq, A_Q), lambda i, j: (i, j, 0)),
        compiler_params=_cparams(("parallel", "parallel")),
    )(*args)


class _HgrnDirection:
    def __init__(self, q_ref, lf_ref, i_ref, st_ref, o_ref, *, rev, start, tb):
        self.q_ref, self.lf_ref, self.i_ref, self.st_ref, self.o_ref = q_ref, lf_ref, i_ref, st_ref, o_ref
        self.rev, self.start, self.tb, self.nc = rev, start, tb, tb // GLA_CHUNK

    def gates(self):
        tb, nc, c_len, width = self.tb, self.nc, GLA_CHUNK, HGRN_WIDTH
        block = slice(self.start, self.start + tb)
        self.qs3 = self.q_ref[0, block, :].reshape(nc, c_len, width)
        lf = self.lf_ref[0, block, :]
        self.kf3 = (1.0 - jnp.exp(lf)).reshape(nc, c_len, width)
        self.lf3 = lf.reshape(nc, c_len, width)
        self.iv = self.i_ref[0, block, :]
        r = lax.broadcasted_iota(jnp.int32, (tb, tb), 0)
        c = lax.broadcasted_iota(jnp.int32, (tb, tb), 1)
        same = (r // c_len) == (c // c_len)
        lower = same & (c <= r)
        self.keep = (same & (c >= r)) if self.rev else lower
        tri = _mx(jnp.where(lower, 1.0, 0.0))
        hi, lo = _split(lf)
        self.cum = (jnp.dot(tri, hi, preferred_element_type=F32)
                    + jnp.dot(tri, lo, preferred_element_type=F32)).reshape(nc, c_len, width)

    def first_matmuls(self):
        tb, nc, c_len, width = self.tb, self.nc, GLA_CHUNK, HGRN_WIDTH
        cum = self.cum
        if self.rev:
            w3 = cum[:, c_len - 1:c_len, :] - cum + self.lf3
            edge = w3[:, 0:1, :]
        else:
            w3 = cum
            edge = cum[:, c_len - 1:c_len, :]
        mid = w3[:, c_len // 2:c_len // 2 + 1, :]
        qa3 = self.qs3 * jnp.exp(w3 - mid)
        kb3 = self.kf3 * jnp.exp(mid - w3)
        qa = _mx(qa3.reshape(tb, width))
        kb = _mx(kb3.reshape(tb, width))
        self.qe = _mx((qa3 * jnp.exp(mid)).reshape(tb, width))
        ke = _mx((kb3 * jnp.exp(edge - mid)).reshape(tb, width))
        self.dec = jnp.exp(edge)
        iv = self.iv
        self.scores, self.kv = [], []
        for h in range(HGRN_HEADS):
            hs = slice(h * HGRN_DK, (h + 1) * HGRN_DK)
            self.scores.append(_dot_nt(qa[:, hs], kb[:, hs]))
            self.kv.append([_dot_tn(iv[ci * c_len:(ci + 1) * c_len, hs], ke[ci * c_len:(ci + 1) * c_len, hs])
                            for ci in range(nc)])

    def second_matmuls(self):
        nc, c_len = self.nc, GLA_CHUNK
        order = range(nc - 1, -1, -1) if self.rev else range(nc)
        for h in range(HGRN_HEADS):
            hs = slice(h * HGRN_DK, (h + 1) * HGRN_DK)
            o_intra = _dot(jnp.where(self.keep, self.scores[h], 0.0), self.iv[:, hs])
            st = self.st_ref[h]
            for ci in order:
                rows = slice(ci * c_len, (ci + 1) * c_len)
                out_rows = slice(self.start + ci * c_len, self.start + (ci + 1) * c_len)
                self.o_ref[0, out_rows, hs] = o_intra[rows] + _dot_nt(self.qe[rows, hs], st)
                st = st * self.dec[ci, :, hs] + self.kv[h][ci]
            self.st_ref[h] = st


def _hgrn_kernel(qf_ref, ff_ref, if_ref, qb_ref, fb_ref, ib_ref, s0f_ref, s0b_ref,
                 of_ref, ob_ref, sff_ref, sfb_ref, st_ref, *, sub):
    j = pl.program_id(1)

    @pl.when(j == 0)
    def _():
        st_ref[0] = s0f_ref[0]
        st_ref[1] = s0b_ref[0]

    units = []
    n_sub = qf_ref.shape[1] // sub
    for k in range(n_sub):
        units.append(_HgrnDirection(qf_ref, ff_ref, if_ref, st_ref.at[0], of_ref, rev=False,
                                    start=k * sub, tb=sub))
        units.append(_HgrnDirection(qb_ref, fb_ref, ib_ref, st_ref.at[1], ob_ref, rev=True,
                                    start=(n_sub - 1 - k) * sub, tb=sub))
    for unit in units:
        unit.gates()
    for unit in units:
        unit.first_matmuls()
    for unit in units:
        unit.second_matmuls()

    @pl.when(j == pl.num_programs(1) - 1)
    def _():
        sff_ref[0] = st_ref[0]
        sfb_ref[0] = st_ref[1]


def _hgrn_scan(hq, hl, hi, s0_fw, s0_bw, *, tb, sub):
    b, l, _ = hq.shape
    nb = l // tb
    fcol = lambda cidx: pl.BlockSpec((1, tb, HGRN_WIDTH), lambda i, j: (i, j, cidx))
    bcol = lambda cidx: pl.BlockSpec((1, tb, HGRN_WIDTH), lambda i, j: (i, nb - 1 - j, cidx))
    st_shape = (HGRN_HEADS, HGRN_DK, HGRN_DK)
    st_spec = pl.BlockSpec((1,) + st_shape, lambda i, j: (i, 0, 0, 0))
    o_shape = jax.ShapeDtypeStruct((b, l, HGRN_WIDTH), F32)
    s_shape = jax.ShapeDtypeStruct((b,) + st_shape, F32)
    return pl.pallas_call(
        functools.partial(_hgrn_kernel, sub=sub),
        out_shape=(o_shape, o_shape, s_shape, s_shape),
        grid=(b, nb),
        in_specs=[fcol(0), fcol(0), fcol(0), bcol(0), bcol(1), bcol(0), st_spec, st_spec],
        out_specs=(fcol(0), bcol(0), st_spec, st_spec),
        scratch_shapes=[pltpu.VMEM((2,) + st_shape, F32)],
        compiler_params=_cparams(("parallel", "arbitrary")),
    )(hq, hl, hi, hq, hl, hi, s0_fw, s0_bw)


def _swiglu_residual(x1, mod_ref, ng2_ref, wi_ref, wd_ref):
    h2 = _mx(_norm_mod(x1, ng2_ref[...], mod_ref[3:4, :], mod_ref[4:5, :]))
    acc = None
    for lo in range(0, FFN_HIDDEN, FFN_CHUNK):
        gate = jnp.dot(h2, wi_ref[:, lo:lo + FFN_CHUNK], preferred_element_type=F32)
        up = jnp.dot(h2, wi_ref[:, FFN_HIDDEN + lo:FFN_HIDDEN + lo + FFN_CHUNK], preferred_element_type=F32)
        part = _dot(_silu(gate) * up, wd_ref[lo:lo + FFN_CHUNK, :])
        acc = part if acc is None else acc + part
    return x1 + mod_ref[5:6, :] * acc


def _out_kernel(*refs, n_mix, gated, n_split):
    mix_refs = refs[:n_mix]
    rest = refs[n_mix:]
    if gated:
        of_ref, ob_ref, g_ref, og_ref = rest[:4]
        rest = rest[4:]
    x_ref, mod_ref, ng2_ref, wo_ref, wi_ref, wd_ref, o_ref = rest
    size = x_ref.shape[1] // n_split

    def attn_residual(rows):
        mixes = [m_ref[0, rows, :] for m_ref in mix_refs]
        if gated:
            o = of_ref[0, rows, :] + ob_ref[0, rows, :]
            og = og_ref[...]
            parts = []
            for h in range(HGRN_HEADS):
                oh = o[:, h * HGRN_DK:(h + 1) * HGRN_DK]
                parts.append(oh * lax.rsqrt(_mean_rows(oh * oh) + EPS) * og)
            mixes.append(_mx(jnp.concatenate(parts, axis=1) * g_ref[0, rows, :].astype(F32)))
        y, off = None, 0
        for m in mixes:
            w = m.shape[-1]
            part = jnp.dot(m, wo_ref[off:off + w, :], preferred_element_type=F32)
            y = part if y is None else y + part
            off += w
        return x_ref[0, rows, :] + mod_ref[2:3, :] * y

    groups = [slice(k * size, (k + 1) * size) for k in range(n_split)]
    x1 = [attn_residual(rows) for rows in groups]
    for rows, x1_k in zip(groups, x1):
        o_ref[0, rows, :] = _swiglu_residual(x1_k, mod_ref, ng2_ref, wi_ref, wd_ref)


def _out_layer(mixes, x, mod4, layer, row_of_batch, ng2, wo, wi, wd, *, tm, gated=None, n_split=1):
    b, l, d = x.shape
    row = lambda w: pl.BlockSpec((1, tm, w), lambda i, j: (i, j, 0))
    in_specs = [row(m.shape[-1]) for m in mixes]
    args = list(mixes)
    if gated is not None:
        in_specs += [row(HGRN_WIDTH), row(HGRN_WIDTH), row(HGRN_WIDTH), _const_spec((1, HGRN_DK))]
        args += list(gated)
    in_specs += [row(d),
                 pl.BlockSpec((None, None, 6, d), lambda i, j: (layer, row_of_batch(i), 0, 0)),
                 _const_spec((1, d)), _const_spec(wo.shape), _const_spec(wi.shape), _const_spec(wd.shape)]
    return pl.pallas_call(
        functools.partial(_out_kernel, n_mix=len(mixes), gated=gated is not None, n_split=n_split),
        out_shape=jax.ShapeDtypeStruct((b, l, d), F32),
        grid=(b, l // tm),
        in_specs=in_specs,
        out_specs=row(d),
        compiler_params=_cparams(("parallel", "parallel")),
    )(*args, x, mod4, ng2, wo, wi, wd)


def _rope_ret(u, cs, sn):
    parts = []
    for h in range(RET_HEADS):
        x1 = u[:, h * RET_DK:h * RET_DK + LANES]
        x2 = u[:, h * RET_DK + LANES:(h + 1) * RET_DK]
        parts += [x1 * cs - x2 * sn, x2 * cs + x1 * sn]
    return jnp.concatenate(parts, axis=1)


def _ret_log_gamma(h, rev):
    hh = RET_HEADS - 1 - h if rev else h
    return math.log(1.0 - 2.0 ** (-5.0 - hh))


def _ret_weights(n, lg, rev):
    pos = lax.broadcasted_iota(jnp.int32, (n, 1), 0).astype(F32)
    if rev:
        return jnp.exp(lg * pos), jnp.exp(lg * (n - pos))
    return jnp.exp(lg * (n - 1.0 - pos)), jnp.exp(lg * (pos + 1.0))


def _ret_tables(n, rev):
    pos = np.arange(n, dtype=np.float64)[:, None]
    out = np.zeros((RET_HEADS, 2, n, LANES))
    for h in range(RET_HEADS):
        lg = _ret_log_gamma(h, rev)
        out[h, 0] = np.exp(lg * pos) if rev else np.exp(lg * (n - 1.0 - pos))
        out[h, 1] = np.exp(lg * (n - pos)) if rev else np.exp(lg * (pos + 1.0))
    return jnp.asarray(out, F32)


def _ret_init_state(kc_ref, vc_ref, s_ref, rev):
    lc = kc_ref.shape[1]
    for h in range(RET_HEADS):
        kw, _ = _ret_weights(lc, _ret_log_gamma(h, rev), rev)
        kc = kc_ref[0, :, h * RET_DK:(h + 1) * RET_DK].astype(F32) * kw
        s_ref[h] = _dot_tn(kc, vc_ref[0, :, h * RET_DV:(h + 1) * RET_DV])


def _proj_odd_kernel(x_ref, mod_ref, ng_ref, w_ref, cs_ref, sn_ref, kc_ref, vc_ref, wt_ref,
                     q_ref, k_ref, v_ref, g_ref, ob_ref, s_ref, *, chunk):
    @pl.when(pl.program_id(1) == 0)
    def _():
        _ret_init_state(kc_ref, vc_ref, s_ref, True)

    h = _mx(_norm_mod(x_ref[0], ng_ref[...], mod_ref[0:1, :], mod_ref[1:2, :]))
    cs, sn = cs_ref[...], sn_ref[...]
    g = jnp.dot(h, w_ref[:, 2 * ODD_Q + ODD_V:], preferred_element_type=F32)
    g_ref[0] = _silu(g).astype(g_ref.dtype)
    q = _mx(_rope_ret(jnp.dot(h, w_ref[:, :ODD_Q], preferred_element_type=F32), cs, sn))
    q_ref[0] = q
    k = _rope_ret(jnp.dot(h, w_ref[:, ODD_Q:2 * ODD_Q], preferred_element_type=F32), cs, sn) * RET_DK ** -0.5
    k_ref[0] = k.astype(k_ref.dtype)
    v = _mx(jnp.dot(h, w_ref[:, 2 * ODD_Q:2 * ODD_Q + ODD_V], preferred_element_type=F32))
    v_ref[0] = v
    for ci in range(x_ref.shape[1] // chunk - 1, -1, -1):
        rows = slice(ci * chunk, (ci + 1) * chunk)
        for hd in range(RET_HEADS):
            ks, vs = slice(hd * RET_DK, (hd + 1) * RET_DK), slice(hd * RET_DV, (hd + 1) * RET_DV)
            s = s_ref[hd]
            ob_ref[0, rows, vs] = _scale_rows(_dot(q[rows, ks], s), wt_ref[hd, 1]).astype(ob_ref.dtype)
            s_ref[hd] = (s * math.exp(_ret_log_gamma(hd, True) * chunk)
                         + _dot_tn(_scale_rows(k[rows, ks], wt_ref[hd, 0]), v[rows, vs]))


def _proj_odd(x, mod4, layer, ng, w, cs, sn, kc, vc, *, tm, chunk):
    b, l, d = x.shape
    lc = kc.shape[1]
    nb = l // tm
    row = lambda wd: pl.BlockSpec((1, tm, wd), lambda i, j: (i, nb - 1 - j, 0))
    tab = pl.BlockSpec((tm, LANES), lambda i, j: (nb - 1 - j, 0))
    ctx = lambda wd: pl.BlockSpec((1, lc, wd), lambda i, j: (i, 0, 0))
    wide = jax.ShapeDtypeStruct((b, l, ODD_V), MXU_DTYPE)
    narrow = jax.ShapeDtypeStruct((b, l, ODD_Q), MXU_DTYPE)
    return pl.pallas_call(
        functools.partial(_proj_odd_kernel, chunk=chunk),
        out_shape=(narrow, narrow, wide, wide, wide),
        grid=(b, nb),
        in_specs=[row(d),
                  pl.BlockSpec((None, None, 6, d), lambda i, j: (layer, i, 0, 0)),
                  _const_spec((1, d)), _const_spec((d, ODD_IN)), tab, tab, ctx(ODD_Q), ctx(ODD_V),
                  _const_spec((RET_HEADS, 2, chunk, LANES))],
        out_specs=(row(ODD_Q), row(ODD_Q), row(ODD_V), row(ODD_V), row(ODD_V)),
        scratch_shapes=[pltpu.VMEM((RET_HEADS, RET_DK, RET_DV), F32)],
        compiler_params=_cparams(("parallel", "arbitrary")),
    )(x, mod4, ng, w, cs, sn, kc, vc, _ret_tables(chunk, True))


def _proj_ctx_kv_kernel(x_ref, mod_ref, ng_ref, w_ref, k_ref, v_ref):
    h = _mx(_norm_mod(x_ref[0], ng_ref[...], mod_ref[0:1, :], mod_ref[1:2, :]))
    k = jnp.dot(h, w_ref[:, :ODD_Q], preferred_element_type=F32)
    k_ref[0] = (k * RET_DK ** -0.5).astype(k_ref.dtype)
    v_ref[0] = jnp.dot(h, w_ref[:, ODD_Q:], preferred_element_type=F32).astype(v_ref.dtype)


def _proj_ctx_kv(ctx, mod4, layer, ctx_row, ng, w_kv):
    b, lc, d = ctx.shape
    return pl.pallas_call(
        _proj_ctx_kv_kernel,
        out_shape=(jax.ShapeDtypeStruct((b, lc, ODD_Q), MXU_DTYPE),
                   jax.ShapeDtypeStruct((b, lc, ODD_V), MXU_DTYPE)),
        grid=(b,),
        in_specs=[pl.BlockSpec((1, lc, d), lambda i: (i, 0, 0)),
                  pl.BlockSpec((None, None, 6, d), lambda i: (layer, ctx_row, 0, 0)),
                  _const_spec((1, d)), _const_spec((d, ODD_Q + ODD_V))],
        out_specs=(pl.BlockSpec((1, lc, ODD_Q), lambda i: (i, 0, 0)),
                   pl.BlockSpec((1, lc, ODD_V), lambda i: (i, 0, 0))),
        compiler_params=_cparams(("parallel",)),
    )(ctx, mod4, ng, w_kv)


def _ret_kernel(q_ref, k_ref, v_ref, kc_ref, vc_ref, dm_ref, wt_ref, ob_ref, sg_ref, o_ref, s_ref, *, chunk):
    @pl.when(pl.program_id(1) == 0)
    def _():
        _ret_init_state(kc_ref, vc_ref, s_ref, False)

    state = [s_ref[h] for h in range(RET_HEADS)]
    staged = []
    for ci in range(q_ref.shape[1] // chunk):
        rows = slice(ci * chunk, (ci + 1) * chunk)
        for h in range(RET_HEADS):
            q = q_ref[0, rows, h * RET_DK:(h + 1) * RET_DK]
            k = k_ref[0, rows, h * RET_DK:(h + 1) * RET_DK]
            v = v_ref[0, rows, h * RET_DV:(h + 1) * RET_DV]
            o = _scale_rows(_dot(q, state[h]), wt_ref[h, 1])
            sc = _dot_nt(q, k)
            state[h] = (state[h] * math.exp(_ret_log_gamma(h, False) * chunk)
                        + _dot_tn(_scale_rows(k.astype(F32), wt_ref[h, 0]), v))
            staged.append((rows, h, o, sc, v))
    for h in range(RET_HEADS):
        s_ref[h] = state[h]
    for rows, h, o, sc, v in staged:
        vs = slice(h * RET_DV, (h + 1) * RET_DV)
        o = o + _dot(sc * dm_ref[h], v) + ob_ref[0, rows, vs].astype(F32)
        o = o * lax.rsqrt(_mean_rows(o * o) + EPS)
        o_ref[0, rows, vs] = (o * sg_ref[0, rows, vs].astype(F32)).astype(o_ref.dtype)


def _ret_decay_matrix(tb):
    rel = np.arange(tb)[:, None] - np.arange(tb)[None, :]
    out = np.zeros((RET_HEADS, tb, tb), np.float64)
    for h in range(RET_HEADS):
        fw = np.exp(_ret_log_gamma(h, False) * np.maximum(rel, 0))
        bw = np.exp(_ret_log_gamma(h, True) * np.maximum(-rel, 0))
        out[h] = np.where(rel > 0, fw, np.where(rel < 0, bw, 2.0))
    return jnp.asarray(out, F32)


def _ret_scan(q, k, v, kc, vc, o_bw, sg, *, tb, chunk):
    b, l, _ = q.shape
    lc = kc.shape[1]
    row = lambda w: pl.BlockSpec((1, tb, w), lambda i, j: (i, j, 0))
    ctx = lambda w: pl.BlockSpec((1, lc, w), lambda i, j: (i, 0, 0))
    return pl.pallas_call(
        functools.partial(_ret_kernel, chunk=chunk),
        out_shape=jax.ShapeDtypeStruct((b, l, ODD_V), MXU_DTYPE),
        grid=(b, l // tb),
        in_specs=[row(ODD_Q), row(ODD_Q), row(ODD_V), ctx(ODD_Q), ctx(ODD_V),
                  _const_spec((RET_HEADS, chunk, chunk)), _const_spec((RET_HEADS, 2, chunk, LANES)),
                  row(ODD_V), row(ODD_V)],
        out_specs=row(ODD_V),
        scratch_shapes=[pltpu.VMEM((RET_HEADS, RET_DK, RET_DV), F32)],
        compiler_params=_cparams(("parallel", "arbitrary")),
    )(q, k, v, kc, vc, _ret_decay_matrix(chunk), _ret_tables(chunk, False), o_bw, sg)


def _axial_tables(n_tok):
    t = np.arange(n_tok)
    n_freq = HEAD_DIM // 4
    inv = ROPE_BASE ** (-np.arange(n_freq, dtype=np.float64) / n_freq)
    ang = np.concatenate([(t // GRID_W)[:, None] * inv, (t % GRID_W)[:, None] * inv], axis=-1)
    cos, sin = np.cos(ang), np.sin(ang)
    cs = np.tile(np.concatenate([cos, cos], axis=-1), (1, LANES // HEAD_DIM))
    sn = np.tile(np.concatenate([-sin, sin], axis=-1), (1, LANES // HEAD_DIM))
    return jnp.asarray(cs, F32), jnp.asarray(sn, F32)


def _retention_tables(n_tok):
    theta = 1.0 / (RET_BASE ** np.linspace(0.0, 1.0, RET_DK // 2))
    ang = np.arange(n_tok, dtype=np.float64)[:, None] * theta
    return jnp.asarray(np.cos(ang), F32), jnp.asarray(np.sin(ang), F32)


def _block_diag_mean(width, head):
    idx = np.arange(width) // head
    return jnp.asarray((idx[:, None] == idx[None, :]) / head, MXU_DTYPE)


def _pick(n, pref):
    while n % pref:
        pref //= 2
    return pref


def kernel(x, c, ctx, c_ctx, mod_w, mod_b, norm_g, ffn_w_in, ffn_w_out, even_w_in, even_w_out,
           attn_qk_norm_g, attn_sink, hgrn_out_norm_g, hgrn_lb, odd_w_in, odd_w_out):
    n_b, n_tok, d = x.shape
    n_ctx = ctx.shape[1]
    depth = mod_w.shape[0]
    assert depth == 2 and d == D_MODEL

    rows = -(-(n_b + 1) // 8) * 8
    cond = jnp.zeros((rows, d), F32).at[:n_b].set(c).at[n_b].set(c_ctx)
    mod4 = _modulation(cond, mod_w, mod_b).reshape(depth, rows, 6, d)
    lat_row = lambda i: i
    ctx_row = lambda i: n_b

    ng1, ng2 = norm_g[0, 0][None, :], norm_g[0, 1][None, :]
    w_in = _mx(even_w_in[0])
    gq = jnp.tile(attn_qk_norm_g[0, 0], ATTN_HEADS)[None, :]
    gk = jnp.tile(attn_qk_norm_g[0, 1], ATTN_KV_HEADS)[None, :]
    bdq, bdk = _block_diag_mean(A_Q, HEAD_DIM), _block_diag_mean(A_KV, HEAD_DIM)
    cs_a, sn_a = _axial_tables(n_tok)
    lower = jnp.cumsum(jax.nn.softmax(hgrn_lb.astype(F32), axis=0), axis=0)[0][None, :]
    og = hgrn_out_norm_g[0][None, :]
    wo = _mx(even_w_out[0])
    wi, wd = _mx(ffn_w_in[0]), _mx(ffn_w_out[0])

    n_flat = n_b * n_ctx
    flat = lambda a: a.reshape(1, n_flat, a.shape[-1])
    unflat = lambda a: a.reshape(n_b, n_ctx, a.shape[-1])
    tm_c = _pick(n_flat, 1024)
    ones_c = jnp.ones((n_flat, LANES), F32)
    qc, kc, vc, hqc, hlc, hic, hgc = map(unflat, _proj_even(
        flat(ctx), mod4, 0, ctx_row, ng1, w_in, gq, gk, bdq, bdk, ones_c, ones_c, lower,
        rope=False, tm=tm_c, n_split=2))
    ql, kl, vl, hql, hll, hil, hgl = _proj_even(x, mod4, 0, lat_row, ng1, w_in, gq, gk, bdq, bdk, cs_a, sn_a,
                                                lower, rope=True, tm=_pick(n_tok, 1024), n_split=2)

    a_ctx = _attention(qc, None, None, kc, vc, attn_sink[0], window=False, tq=WINDOW)
    a_lat = _attention(ql, kl, vl, kc, vc, attn_sink[0], window=True, tq=_pick(n_tok, 4 * WINDOW))

    zeros = jnp.zeros((n_b, HGRN_HEADS, HGRN_DK, HGRN_DK), F32)
    sub_c, sub_l = _pick(n_ctx, 256), _pick(n_tok, 256)
    o_fw_c, o_bw_c, s_fw, s_bw = _hgrn_scan(hqc, hlc, hic, zeros, zeros, tb=sub_c, sub=sub_c)
    o_fw, o_bw, _, _ = _hgrn_scan(hql, hll, hil, s_fw, s_bw, tb=_pick(n_tok, 2 * sub_l), sub=sub_l)

    ctx1 = unflat(_out_layer([flat(a_ctx)], flat(ctx), mod4, 0, ctx_row, ng2, wo, wi, wd, tm=tm_c, n_split=2,
                             gated=(flat(o_fw_c), flat(o_bw_c), flat(hgc), og)))
    x1 = _out_layer([a_lat], x, mod4, 0, lat_row, ng2, wo, wi, wd, tm=_pick(n_tok, 1024), n_split=2,
                    gated=(o_fw, o_bw, hgl, og))

    ng1, ng2 = norm_g[1, 0][None, :], norm_g[1, 1][None, :]
    w_in = _mx(odd_w_in[0])
    cs_r, sn_r = _retention_tables(n_tok)
    wo = _mx(odd_w_out[0])
    wi, wd = _mx(ffn_w_in[1]), _mx(ffn_w_out[1])

    kc, vc = map(unflat, _proj_ctx_kv(ctx1.reshape(n_flat // tm_c, tm_c, d), mod4, 1, n_b, ng1,
                                      w_in[:, ODD_Q:2 * ODD_Q + ODD_V]))
    tb_r = _pick(n_tok, 256)
    q, k, v, sg, o_bw = _proj_odd(x1, mod4, 1, ng1, w_in, cs_r, sn_r, kc, vc, tm=_pick(n_tok, 512), chunk=tb_r)
    mix = _ret_scan(q, k, v, kc, vc, o_bw, sg, tb=_pick(n_tok, 512), chunk=tb_r)
    return _out_layer([mix], x1, mod4, 1, lat_row, ng2, wo, wi, wd, tm=_pick(n_tok, 1024), n_split=2)
```

```python
import functools
import math

import numpy as np
import jax
import jax.numpy as jnp
from jax import lax
from jax.experimental import pallas as pl
from jax.experimental.pallas import tpu as pltpu

F32 = jnp.float32
MXU_DTYPE = jnp.bfloat16

D_MODEL = 1024
GRID_W = 64
HEAD_DIM = 64
ATTN_HEADS = 8
ATTN_KV_HEADS = 2
WINDOW = 128
ROPE_BASE = 10000.0
HGRN_HEADS = 4
HGRN_DK = 128
HGRN_WIDTH = HGRN_HEADS * HGRN_DK
GLA_CHUNK = 64
RET_HEADS = 4
RET_DK = 256
RET_DV = 512
RET_BASE = 10000.0
FFN_HIDDEN = 2816
A_Q = ATTN_HEADS * HEAD_DIM
A_KV = ATTN_KV_HEADS * HEAD_DIM
EVEN_IN = A_Q + 2 * A_KV + 5 * HGRN_WIDTH
HG_OFF = A_Q + 2 * A_KV
ODD_Q = RET_HEADS * RET_DK
ODD_V = RET_HEADS * RET_DV
ODD_IN = 2 * ODD_Q + 2 * ODD_V
EPS = 1e-6
NEG = -1e30
LOG2E = 1.4426950408889634

LANES = 128
MXU_TILE = 256
FFN_CHUNK = MXU_TILE
VMEM_LIMIT = 56 * 1024 * 1024


def _mx(a):
    return a.astype(MXU_DTYPE)


def _dot(a, b):
    return jnp.dot(_mx(a), _mx(b), preferred_element_type=F32)


def _dot_nt(a, b):
    return lax.dot_general(_mx(a), _mx(b), (((1,), (1,)), ((), ())), preferred_element_type=F32)


def _dot_tn(a, b):
    return lax.dot_general(_mx(a), _mx(b), (((0,), (0,)), ((), ())), preferred_element_type=F32)


def _split(x):
    hi = _mx(x)
    lo = _mx(x - hi.astype(F32))
    return hi, lo


def _sigmoid(x):
    return 1.0 / (1.0 + jnp.exp(-x))


def _silu(x):
    return x * _sigmoid(x)


def _norm_mod(x, g, shift, scale):
    y = x * lax.rsqrt(_mean_rows(x * x) + EPS) * g
    return y * (1.0 + scale) + shift


def _lane_tiles(x):
    return [x[:, t * LANES:(t + 1) * LANES] for t in range(x.shape[1] // LANES)]


def _scale_rows(x, tab):
    return jnp.concatenate([t * tab for t in _lane_tiles(x)], axis=1)


def _mean_rows(x):
    return jnp.sum(functools.reduce(jnp.add, _lane_tiles(x)), axis=-1, keepdims=True) * (1.0 / x.shape[1])


def _cparams(sem):
    return pltpu.CompilerParams(dimension_semantics=sem, vmem_limit_bytes=VMEM_LIMIT)


def _const_spec(shape):
    nd = len(shape)
    return pl.BlockSpec(shape, lambda *_: (0,) * nd, pipeline_mode=pl.Buffered(1))


def _mod_kernel(c_ref, w_ref, b_ref, o_ref):
    s = _silu(c_ref[...])
    o_ref[0] = jnp.dot(s, w_ref[0], preferred_element_type=F32,
                       precision=lax.Precision.HIGHEST) + b_ref[0]


def _modulation(cond, mod_w, mod_b):
    depth, d, n = mod_w.shape
    rows = cond.shape[0]
    tn = 1536
    return pl.pallas_call(
        _mod_kernel,
        out_shape=jax.ShapeDtypeStruct((depth, rows, n), F32),
        grid=(depth, n // tn),
        in_specs=[pl.BlockSpec((rows, d), lambda l, j: (0, 0)),
                  pl.BlockSpec((1, d, tn), lambda l, j: (l, 0, j)),
                  pl.BlockSpec((1, 1, tn), lambda l, j: (l, 0, j))],
        out_specs=pl.BlockSpec((1, rows, tn), lambda l, j: (l, 0, j)),
        compiler_params=_cparams(("arbitrary", "arbitrary")),
    )(cond, mod_w, mod_b.reshape(depth, 1, n))


def _head_rms(u, bd, gain):
    hi, lo = _split(u * u)
    width = u.shape[1]
    step = min(width, MXU_TILE)
    ms = jnp.concatenate(
        [jnp.dot(hi[:, c:c + step], bd[c:c + step, c:c + step], preferred_element_type=F32)
         + jnp.dot(lo[:, c:c + step], bd[c:c + step, c:c + step], preferred_element_type=F32)
         for c in range(0, width, step)], axis=1)
    return u * lax.rsqrt(ms + EPS) * gain


def _rope_half_swap(u):
    lane = lax.broadcasted_iota(jnp.int32, u.shape, 1)
    first = (lane % HEAD_DIM) < (HEAD_DIM // 2)
    return jnp.where(first, pltpu.roll(u, LANES - HEAD_DIM // 2, 1), pltpu.roll(u, HEAD_DIM // 2, 1))


def _dup_halves(u):
    lane = lax.broadcasted_iota(jnp.int32, u.shape, 1)
    low = lane < HEAD_DIM
    swapped = pltpu.roll(u, HEAD_DIM, 1)
    return jnp.concatenate([jnp.where(low, u, swapped), jnp.where(low, swapped, u)], axis=1)


def _proj_even_kernel(x_ref, mod_ref, ng_ref, w_ref, gq_ref, gk_ref, bdq_ref, bdk_ref, cs_ref, sn_ref, lb_ref,
                      q_ref, k_ref, v_ref, hq_ref, hl_ref, hi_ref, hg_ref, *, rope, n_split):
    lb = lb_ref[...]
    col = lambda c: w_ref[:, HG_OFF + c * HGRN_WIDTH:HG_OFF + (c + 1) * HGRN_WIDTH]
    log_gate = lambda raw: jnp.log(lb + (1.0 - lb) * _sigmoid(raw))
    size = x_ref.shape[1] // n_split
    for grp in range(n_split):
        rows = slice(grp * size, (grp + 1) * size)
        h = _mx(_norm_mod(x_ref[0, rows, :], ng_ref[...], mod_ref[0:1, :], mod_ref[1:2, :]))
        hl_ref[0, rows, :HGRN_WIDTH] = log_gate(jnp.dot(h, col(1), preferred_element_type=F32))
        hl_ref[0, rows, HGRN_WIDTH:] = log_gate(jnp.dot(h, col(2), preferred_element_type=F32))
        qkv = jnp.dot(h, w_ref[:, :HG_OFF], preferred_element_type=F32)
        hg_ref[0, rows, :] = _silu(jnp.dot(h, col(4), preferred_element_type=F32)).astype(hg_ref.dtype)
        q = _head_rms(qkv[:, :A_Q], bdq_ref[...], gq_ref[...])
        k = _head_rms(qkv[:, A_Q:A_Q + A_KV], bdk_ref[...], gk_ref[...])
        hq_ref[0, rows, :] = _silu(jnp.dot(h, col(0), preferred_element_type=F32))
        hi_ref[0, rows, :] = jnp.dot(h, col(3), preferred_element_type=F32).astype(hi_ref.dtype)
        v = qkv[:, A_Q + A_KV:HG_OFF]
        if rope:
            cs, sn = cs_ref[rows, :], sn_ref[rows, :]
            q = jnp.concatenate(
                [q[:, c * LANES:(c + 1) * LANES] * cs + _rope_half_swap(q[:, c * LANES:(c + 1) * LANES]) * sn
                 for c in range(A_Q // LANES)], axis=1)
            k = k * cs + _rope_half_swap(k) * sn
        q_ref[0, rows, :] = (q * (HEAD_DIM ** -0.5 * LOG2E)).astype(q_ref.dtype)
        k_ref[0, rows, :] = _dup_halves(k).astype(k_ref.dtype)
        v_ref[0, rows, :] = _dup_halves(v).astype(v_ref.dtype)


def _proj_even(x, mod4, layer, row_of_batch, ng, w, gq, gk, bdq, bdk, cs, sn, lb, *, rope, tm, n_split=1):
    b, l, d = x.shape
    nblk = l // tm
    kern = functools.partial(_proj_even_kernel, rope=rope, n_split=n_split)
    tab_spec = pl.BlockSpec((tm, LANES), lambda i, j: (j, 0))
    return pl.pallas_call(
        kern,
        out_shape=(jax.ShapeDtypeStruct((b, l, A_Q), MXU_DTYPE),
                   jax.ShapeDtypeStruct((b, l, 2 * A_KV), MXU_DTYPE),
                   jax.ShapeDtypeStruct((b, l, 2 * A_KV), MXU_DTYPE),
                   jax.ShapeDtypeStruct((b, l, HGRN_WIDTH), F32),
                   jax.ShapeDtypeStruct((b, l, 2 * HGRN_WIDTH), F32),
                   jax.ShapeDtypeStruct((b, l, HGRN_WIDTH), MXU_DTYPE),
                   jax.ShapeDtypeStruct((b, l, HGRN_WIDTH), MXU_DTYPE)),
        grid=(b, nblk),
        in_specs=[pl.BlockSpec((1, tm, d), lambda i, j: (i, j, 0)),
                  pl.BlockSpec((None, None, 6, d), lambda i, j: (layer, row_of_batch(i), 0, 0)),
                  _const_spec((1, d)),
                  _const_spec((d, EVEN_IN)),
                  _const_spec((1, A_Q)), _const_spec((1, A_KV)),
                  _const_spec((A_Q, A_Q)), _const_spec((A_KV, A_KV)),
                  tab_spec, tab_spec, _const_spec((1, HGRN_WIDTH))],
        out_specs=(pl.BlockSpec((1, tm, A_Q), lambda i, j: (i, j, 0)),
                   pl.BlockSpec((1, tm, 2 * A_KV), lambda i, j: (i, j, 0)),
                   pl.BlockSpec((1, tm, 2 * A_KV), lambda i, j: (i, j, 0)),
                   pl.BlockSpec((1, tm, HGRN_WIDTH), lambda i, j: (i, j, 0)),
                   pl.BlockSpec((1, tm, 2 * HGRN_WIDTH), lambda i, j: (i, j, 0)),
                   pl.BlockSpec((1, tm, HGRN_WIDTH), lambda i, j: (i, j, 0)),
                   pl.BlockSpec((1, tm, HGRN_WIDTH), lambda i, j: (i, j, 0))),
        compiler_params=_cparams(("parallel", "parallel")),
    )(x, mod4, ng, w, gq, gk, bdq, bdk, cs, sn, lb)


def _attn_kernel(*refs, window, tq, seq):
    if window:
        (q_ref, kp_ref, kc_ref, kn_ref, vp_ref, vc_ref, vn_ref, kx_ref, vx_ref, sink_ref, o_ref) = refs
    else:
        (q_ref, kx_ref, vx_ref, sink_ref, o_ref) = refs
    i = pl.program_id(1)
    n_ctx = kx_ref.shape[1]
    sub = WINDOW
    span = 3 * sub
    lane = lax.broadcasted_iota(jnp.int32, (2 * sub, LANES), 1)
    low = lane < HEAD_DIM
    if window:
        rr = lax.broadcasted_iota(jnp.int32, (sub, sub), 0)
        cc = lax.broadcasted_iota(jnp.int32, (sub, sub), 1)
    ones = jnp.ones((n_ctx + (span if window else 0), LANES), MXU_DTYPE)

    def scores(sb, g):
        rows = slice(sb * sub, (sb + 1) * sub)
        cols = slice(g * LANES, (g + 1) * LANES)
        if window:
            kfull = jnp.concatenate([kp_ref[0, :, cols], kc_ref[0, :, cols], kn_ref[0, :, cols]], axis=0)
            kk = jnp.concatenate([kfull[sb * sub:sb * sub + span], kx_ref[0, :, cols]], axis=0)
        else:
            kk = kx_ref[0, :, cols]
        q2 = jnp.concatenate([q_ref[0, rows, (2 * g) * LANES:(2 * g + 1) * LANES],
                              q_ref[0, rows, (2 * g + 1) * LANES:(2 * g + 2) * LANES]], axis=0)
        zero = jnp.zeros_like(q2)
        q4 = jnp.concatenate([jnp.where(low, q2, zero), jnp.where(low, zero, q2)], axis=0)
        return _dot_nt(q4, kk)

    def finish(sb, g, s):
        rows = slice(sb * sub, (sb + 1) * sub)
        cols = slice(g * LANES, (g + 1) * LANES)
        tiles = [s[:, t * LANES:(t + 1) * LANES] for t in range(s.shape[1] // LANES)]
        if window:
            base = i * tq + sb * sub
            ok_prev = (cc >= rr) & (base - sub + cc >= 0)
            ok_next = (cc <= rr) & (base + sub + cc < seq)
            tiles[0] = tiles[0] + jnp.concatenate([jnp.where(ok_prev, 0.0, NEG)] * 4, axis=0)
            tiles[2] = tiles[2] + jnp.concatenate([jnp.where(ok_next, 0.0, NEG)] * 4, axis=0)
            s = jnp.concatenate(tiles, axis=1)
            vfull = jnp.concatenate([vp_ref[0, :, cols], vc_ref[0, :, cols], vn_ref[0, :, cols]], axis=0)
            vv = jnp.concatenate([vfull[sb * sub:sb * sub + span], vx_ref[0, :, cols]], axis=0)
        else:
            vv = vx_ref[0, :, cols]
        tmax = functools.reduce(jnp.maximum, tiles)
        sink = sink_ref[g]
        m = jnp.maximum(jnp.max(tmax, axis=-1, keepdims=True), sink)
        p = jnp.exp2(s - m)
        pv = _dot(p, jnp.concatenate([vv, ones], axis=1))
        o = pv[:, :LANES] * (1.0 / (pv[:, LANES:] + jnp.exp2(sink - m)))
        o2 = jnp.where(low, o[:2 * sub], o[2 * sub:])
        o_ref[0, rows, (2 * g) * LANES:(2 * g + 1) * LANES] = o2[:sub].astype(o_ref.dtype)
        o_ref[0, rows, (2 * g + 1) * LANES:(2 * g + 2) * LANES] = o2[sub:].astype(o_ref.dtype)

    units = [(sb, g) for sb in range(tq // sub) for g in range(ATTN_KV_HEADS)]
    ahead = 1
    pending = [scores(*u) for u in units[:ahead]]
    for idx, unit in enumerate(units):
        if idx + ahead < len(units):
            pending.append(scores(*units[idx + ahead]))
        finish(*unit, pending.pop(0))


def _sink_rows(sink, tq):
    s = sink.astype(F32).reshape(ATTN_KV_HEADS, 4) * LOG2E
    order = jnp.array([0, 2, 1, 3])
    return jnp.repeat(s[:, order], tq, axis=1)[:, :, None]


def _attention(q, k, v, kx, vx, sink, *, window, tq):
    b, l, _ = q.shape
    n_ctx = kx.shape[1]
    nq = l // tq
    kern = functools.partial(_attn_kernel, window=window, tq=tq, seq=l)
    qspec = pl.BlockSpec((1, tq, A_Q), lambda i, j: (i, j, 0))
    xspec = pl.BlockSpec((1, n_ctx, 2 * A_KV), lambda i, j: (i, 0, 0))
    sspec = _const_spec((ATTN_KV_HEADS, 4 * WINDOW, 1))
    if window:
        r = tq // WINDOW
        prev = pl.BlockSpec((1, WINDOW, 2 * A_KV), lambda i, j: (i, jnp.maximum(j * r - 1, 0), 0))
        cur = pl.BlockSpec((1, tq, 2 * A_KV), lambda i, j: (i, j, 0))
        nxt = pl.BlockSpec((1, WINDOW, 2 * A_KV), lambda i, j: (i, jnp.minimum((j + 1) * r, nq * r - 1), 0))
        in_specs = [qspec, prev, cur, nxt, prev, cur, nxt, xspec, xspec, sspec]
        args = (q, k, k, k, v, v, v, kx, vx, _sink_rows(sink, WINDOW))
    else:
        in_specs = [qspec, xspec, xspec, sspec]
        args = (q, kx, vx, _sink_rows(sink, WINDOW))
    return pl.pallas_call(
        kern,
        out_shape=jax.ShapeDtypeStruct((b, l, A_Q), MXU_DTYPE),
        grid=(b, nq),
        in_specs=in_specs,
        out_specs=pl.BlockSpec((1, tq, A_Q), lambda i, j: (i, j, 0)),
        compiler_params=_cparams(("parallel", "parallel")),
    )(*args)


class _HgrnDirection:
    def __init__(self, q_ref, lf_ref, i_ref, st_ref, o_ref, *, rev, start, tb):
        self.q_ref, self.lf_ref, self.i_ref, self.st_ref, self.o_ref = q_ref, lf_ref, i_ref, st_ref, o_ref
        self.rev, self.start, self.tb, self.nc = rev, start, tb, tb // GLA_CHUNK

    def gates(self):
        tb, nc, c_len, width = self.tb, self.nc, GLA_CHUNK, HGRN_WIDTH
        block = slice(self.start, self.start + tb)
        self.qs3 = self.q_ref[0, block, :].reshape(nc, c_len, width)
        lf = self.lf_ref[0, block, :]
        self.kf3 = (1.0 - jnp.exp(lf)).reshape(nc, c_len, width)
        self.lf3 = lf.reshape(nc, c_len, width)
        self.iv = self.i_ref[0, block, :]
        r = lax.broadcasted_iota(jnp.int32, (tb, tb), 0)
        c = lax.broadcasted_iota(jnp.int32, (tb, tb), 1)
        same = (r // c_len) == (c // c_len)
        lower = same & (c <= r)
        self.keep = (same & (c >= r)) if self.rev else lower
        tri = _mx(jnp.where(lower, 1.0, 0.0))
        hi, lo = _split(lf)
        self.cum = (jnp.dot(tri, hi, preferred_element_type=F32)
                    + jnp.dot(tri, lo, preferred_element_type=F32)).reshape(nc, c_len, width)

    def first_matmuls(self):
        tb, nc, c_len, width = self.tb, self.nc, GLA_CHUNK, HGRN_WIDTH
        cum = self.cum
        if self.rev:
            w3 = cum[:, c_len - 1:c_len, :] - cum + self.lf3
            edge = w3[:, 0:1, :]
        else:
            w3 = cum
            edge = cum[:, c_len - 1:c_len, :]
        mid = w3[:, c_len // 2:c_len // 2 + 1, :]
        qa3 = self.qs3 * jnp.exp(w3 - mid)
        kb3 = self.kf3 * jnp.exp(mid - w3)
        qa = _mx(qa3.reshape(tb, width))
        kb = _mx(kb3.reshape(tb, width))
        self.qe = _mx((qa3 * jnp.exp(mid)).reshape(tb, width))
        ke = _mx((kb3 * jnp.exp(edge - mid)).reshape(tb, width))
        self.dec = jnp.exp(edge)
        iv = self.iv
        self.scores, self.kv = [], []
        for h in range(HGRN_HEADS):
            hs = slice(h * HGRN_DK, (h + 1) * HGRN_DK)
            self.scores.append(_dot_nt(qa[:, hs], kb[:, hs]))
            self.kv.append([_dot_tn(iv[ci * c_len:(ci + 1) * c_len, hs], ke[ci * c_len:(ci + 1) * c_len, hs])
                            for ci in range(nc)])

    def second_matmuls(self):
        nc, c_len = self.nc, GLA_CHUNK
        order = range(nc - 1, -1, -1) if self.rev else range(nc)
        for h in range(HGRN_HEADS):
            hs = slice(h * HGRN_DK, (h + 1) * HGRN_DK)
            o_intra = _dot(jnp.where(self.keep, self.scores[h], 0.0), self.iv[:, hs])
            st = self.st_ref[h]
            for ci in order:
                rows = slice(ci * c_len, (ci + 1) * c_len)
                out_rows = slice(self.start + ci * c_len, self.start + (ci + 1) * c_len)
                self.o_ref[0, out_rows, hs] = o_intra[rows] + _dot_nt(self.qe[rows, hs], st)
                st = st * self.dec[ci, :, hs] + self.kv[h][ci]
            self.st_ref[h] = st


def _hgrn_kernel(qf_ref, ff_ref, if_ref, qb_ref, fb_ref, ib_ref, s0f_ref, s0b_ref,
                 of_ref, ob_ref, sff_ref, sfb_ref, st_ref, *, sub):
    j = pl.program_id(1)

    @pl.when(j == 0)
    def _():
        st_ref[0] = s0f_ref[0]
        st_ref[1] = s0b_ref[0]

    units = []
    n_sub = qf_ref.shape[1] // sub
    for k in range(n_sub):
        units.append(_HgrnDirection(qf_ref, ff_ref, if_ref, st_ref.at[0], of_ref, rev=False,
                                    start=k * sub, tb=sub))
        units.append(_HgrnDirection(qb_ref, fb_ref, ib_ref, st_ref.at[1], ob_ref, rev=True,
                                    start=(n_sub - 1 - k) * sub, tb=sub))
    for unit in units:
        unit.gates()
    for unit in units:
        unit.first_matmuls()
    for unit in units:
        unit.second_matmuls()

    @pl.when(j == pl.num_programs(1) - 1)
    def _():
        sff_ref[0] = st_ref[0]
        sfb_ref[0] = st_ref[1]


def _hgrn_scan(hq, hl, hi, s0_fw, s0_bw, *, tb, sub):
    b, l, _ = hq.shape
    nb = l // tb
    fcol = lambda cidx: pl.BlockSpec((1, tb, HGRN_WIDTH), lambda i, j: (i, j, cidx))
    bcol = lambda cidx: pl.BlockSpec((1, tb, HGRN_WIDTH), lambda i, j: (i, nb - 1 - j, cidx))
    st_shape = (HGRN_HEADS, HGRN_DK, HGRN_DK)
    st_spec = pl.BlockSpec((1,) + st_shape, lambda i, j: (i, 0, 0, 0))
    o_shape = jax.ShapeDtypeStruct((b, l, HGRN_WIDTH), F32)
    s_shape = jax.ShapeDtypeStruct((b,) + st_shape, F32)
    return pl.pallas_call(
        functools.partial(_hgrn_kernel, sub=sub),
        out_shape=(o_shape, o_shape, s_shape, s_shape),
        grid=(b, nb),
        in_specs=[fcol(0), fcol(0), fcol(0), bcol(0), bcol(1), bcol(0), st_spec, st_spec],
        out_specs=(fcol(0), bcol(0), st_spec, st_spec),
        scratch_shapes=[pltpu.VMEM((2,) + st_shape, F32)],
        compiler_params=_cparams(("parallel", "arbitrary")),
    )(hq, hl, hi, hq, hl, hi, s0_fw, s0_bw)


def _swiglu_residual(x1, mod_ref, ng2_ref, wi_ref, wd_ref):
    h2 = _mx(_norm_mod(x1, ng2_ref[...], mod_ref[3:4, :], mod_ref[4:5, :]))
    acc = None
    for lo in range(0, FFN_HIDDEN, FFN_CHUNK):
        gate = jnp.dot(h2, wi_ref[:, lo:lo + FFN_CHUNK], preferred_element_type=F32)
        up = jnp.dot(h2, wi_ref[:, FFN_HIDDEN + lo:FFN_HIDDEN + lo + FFN_CHUNK], preferred_element_type=F32)
        part = _dot(_silu(gate) * up, wd_ref[lo:lo + FFN_CHUNK, :])
        acc = part if acc is None else acc + part
    return x1 + mod_ref[5:6, :] * acc


def _out_kernel(*refs, n_mix, gated, n_split):
    mix_refs = refs[:n_mix]
    rest = refs[n_mix:]
    if gated:
        of_ref, ob_ref, g_ref, og_ref = rest[:4]
        rest = rest[4:]
    x_ref, mod_ref, ng2_ref, wo_ref, wi_ref, wd_ref, o_ref = rest
    size = x_ref.shape[1] // n_split

    def attn_residual(rows):
        mixes = [m_ref[0, rows, :] for m_ref in mix_refs]
        if gated:
            o = of_ref[0, rows, :] + ob_ref[0, rows, :]
            og = og_ref[...]
            parts = []
            for h in range(HGRN_HEADS):
                oh = o[:, h * HGRN_DK:(h + 1) * HGRN_DK]
                parts.append(oh * lax.rsqrt(_mean_rows(oh * oh) + EPS) * og)
            mixes.append(_mx(jnp.concatenate(parts, axis=1) * g_ref[0, rows, :].astype(F32)))
        y, off = None, 0
        for m in mixes:
            w = m.shape[-1]
            part = jnp.dot(m, wo_ref[off:off + w, :], preferred_element_type=F32)
            y = part if y is None else y + part
            off += w
        return x_ref[0, rows, :] + mod_ref[2:3, :] * y

    groups = [slice(k * size, (k + 1) * size) for k in range(n_split)]
    x1 = [attn_residual(rows) for rows in groups]
    for rows, x1_k in zip(groups, x1):
        o_ref[0, rows, :] = _swiglu_residual(x1_k, mod_ref, ng2_ref, wi_ref, wd_ref)


def _out_layer(mixes, x, mod4, layer, row_of_batch, ng2, wo, wi, wd, *, tm, gated=None, n_split=1):
    b, l, d = x.shape
    row = lambda w: pl.BlockSpec((1, tm, w), lambda i, j: (i, j, 0))
    in_specs = [row(m.shape[-1]) for m in mixes]
    args = list(mixes)
    if gated is not None:
        in_specs += [row(HGRN_WIDTH), row(HGRN_WIDTH), row(HGRN_WIDTH), _const_spec((1, HGRN_DK))]
        args += list(gated)
    in_specs += [row(d),
                 pl.BlockSpec((None, None, 6, d), lambda i, j: (layer, row_of_batch(i), 0, 0)),
                 _const_spec((1, d)), _const_spec(wo.shape), _const_spec(wi.shape), _const_spec(wd.shape)]
    return pl.pallas_call(
        functools.partial(_out_kernel, n_mix=len(mixes), gated=gated is not None, n_split=n_split),
        out_shape=jax.ShapeDtypeStruct((b, l, d), F32),
        grid=(b, l // tm),
        in_specs=in_specs,
        out_specs=row(d),
        compiler_params=_cparams(("parallel", "parallel")),
    )(*args, x, mod4, ng2, wo, wi, wd)


def _rope_ret(u, cs, sn):
    parts = []
    for h in range(RET_HEADS):
        x1 = u[:, h * RET_DK:h * RET_DK + LANES]
        x2 = u[:, h * RET_DK + LANES:(h + 1) * RET_DK]
        parts += [x1 * cs - x2 * sn, x2 * cs + x1 * sn]
    return jnp.concatenate(parts, axis=1)


def _ret_log_gamma(h, rev):
    hh = RET_HEADS - 1 - h if rev else h
    return math.log(1.0 - 2.0 ** (-5.0 - hh))


def _ret_weights(n, lg, rev):
    pos = lax.broadcasted_iota(jnp.int32, (n, 1), 0).astype(F32)
    if rev:
        return jnp.exp(lg * pos), jnp.exp(lg * (n - pos))
    return jnp.exp(lg * (n - 1.0 - pos)), jnp.exp(lg * (pos + 1.0))


def _ret_tables(n, rev):
    pos = np.arange(n, dtype=np.float64)[:, None]
    out = np.zeros((RET_HEADS, 2, n, LANES))
    for h in range(RET_HEADS):
        lg = _ret_log_gamma(h, rev)
        out[h, 0] = np.exp(lg * pos) if rev else np.exp(lg * (n - 1.0 - pos))
        out[h, 1] = np.exp(lg * (n - pos)) if rev else np.exp(lg * (pos + 1.0))
    return jnp.asarray(out, F32)


def _ret_init_state(kc_ref, vc_ref, s_ref, rev):
    lc = kc_ref.shape[1]
    for h in range(RET_HEADS):
        kw, _ = _ret_weights(lc, _ret_log_gamma(h, rev), rev)
        kc = kc_ref[0, :, h * RET_DK:(h + 1) * RET_DK].astype(F32) * kw
        s_ref[h] = _dot_tn(kc, vc_ref[0, :, h * RET_DV:(h + 1) * RET_DV])


def _proj_odd_kernel(x_ref, mod_ref, ng_ref, w_ref, cs_ref, sn_ref, kc_ref, vc_ref, wt_ref,
                     q_ref, k_ref, v_ref, g_ref, ob_ref, s_ref, *, chunk):
    @pl.when(pl.program_id(1) == 0)
    def _():
        _ret_init_state(kc_ref, vc_ref, s_ref, True)

    h = _mx(_norm_mod(x_ref[0], ng_ref[...], mod_ref[0:1, :], mod_ref[1:2, :]))
    cs, sn = cs_ref[...], sn_ref[...]
    g = jnp.dot(h, w_ref[:, 2 * ODD_Q + ODD_V:], preferred_element_type=F32)
    g_ref[0] = _silu(g).astype(g_ref.dtype)
    q = _mx(_rope_ret(jnp.dot(h, w_ref[:, :ODD_Q], preferred_element_type=F32), cs, sn))
    q_ref[0] = q
    k = _rope_ret(jnp.dot(h, w_ref[:, ODD_Q:2 * ODD_Q], preferred_element_type=F32), cs, sn) * RET_DK ** -0.5
    k_ref[0] = k.astype(k_ref.dtype)
    v = _mx(jnp.dot(h, w_ref[:, 2 * ODD_Q:2 * ODD_Q + ODD_V], preferred_element_type=F32))
    v_ref[0] = v
    for ci in range(x_ref.shape[1] // chunk - 1, -1, -1):
        rows = slice(ci * chunk, (ci + 1) * chunk)
        for hd in range(RET_HEADS):
            ks, vs = slice(hd * RET_DK, (hd + 1) * RET_DK), slice(hd * RET_DV, (hd + 1) * RET_DV)
            s = s_ref[hd]
            ob_ref[0, rows, vs] = _scale_rows(_dot(q[rows, ks], s), wt_ref[hd, 1]).astype(ob_ref.dtype)
            s_ref[hd] = (s * math.exp(_ret_log_gamma(hd, True) * chunk)
                         + _dot_tn(_scale_rows(k[rows, ks], wt_ref[hd, 0]), v[rows, vs]))


def _proj_odd(x, mod4, layer, ng, w, cs, sn, kc, vc, *, tm, chunk):
    b, l, d = x.shape
    lc = kc.shape[1]
    nb = l // tm
    row = lambda wd: pl.BlockSpec((1, tm, wd), lambda i, j: (i, nb - 1 - j, 0))
    tab = pl.BlockSpec((tm, LANES), lambda i, j: (nb - 1 - j, 0))
    ctx = lambda wd: pl.BlockSpec((1, lc, wd), lambda i, j: (i, 0, 0))
    wide = jax.ShapeDtypeStruct((b, l, ODD_V), MXU_DTYPE)
    narrow = jax.ShapeDtypeStruct((b, l, ODD_Q), MXU_DTYPE)
    return pl.pallas_call(
        functools.partial(_proj_odd_kernel, chunk=chunk),
        out_shape=(narrow, narrow, wide, wide, wide),
        grid=(b, nb),
        in_specs=[row(d),
                  pl.BlockSpec((None, None, 6, d), lambda i, j: (layer, i, 0, 0)),
                  _const_spec((1, d)), _const_spec((d, ODD_IN)), tab, tab, ctx(ODD_Q), ctx(ODD_V),
                  _const_spec((RET_HEADS, 2, chunk, LANES))],
        out_specs=(row(ODD_Q), row(ODD_Q), row(ODD_V), row(ODD_V), row(ODD_V)),
        scratch_shapes=[pltpu.VMEM((RET_HEADS, RET_DK, RET_DV), F32)],
        compiler_params=_cparams(("parallel", "arbitrary")),
    )(x, mod4, ng, w, cs, sn, kc, vc, _ret_tables(chunk, True))


def _proj_ctx_kv_kernel(x_ref, mod_ref, ng_ref, w_ref, k_ref, v_ref):
    h = _mx(_norm_mod(x_ref[0], ng_ref[...], mod_ref[0:1, :], mod_ref[1:2, :]))
    k = jnp.dot(h, w_ref[:, :ODD_Q], preferred_element_type=F32)
    k_ref[0] = (k * RET_DK ** -0.5).astype(k_ref.dtype)
    v_ref[0] = jnp.dot(h, w_ref[:, ODD_Q:], preferred_element_type=F32).astype(v_ref.dtype)


def _proj_ctx_kv(ctx, mod4, layer, ctx_row, ng, w_kv):
    b, lc, d = ctx.shape
    return pl.pallas_call(
        _proj_ctx_kv_kernel,
        out_shape=(jax.ShapeDtypeStruct((b, lc, ODD_Q), MXU_DTYPE),
                   jax.ShapeDtypeStruct((b, lc, ODD_V), MXU_DTYPE)),
        grid=(b,),
        in_specs=[pl.BlockSpec((1, lc, d), lambda i: (i, 0, 0)),
                  pl.BlockSpec((None, None, 6, d), lambda i: (layer, ctx_row, 0, 0)),
                  _const_spec((1, d)), _const_spec((d, ODD_Q + ODD_V))],
        out_specs=(pl.BlockSpec((1, lc, ODD_Q), lambda i: (i, 0, 0)),
                   pl.BlockSpec((1, lc, ODD_V), lambda i: (i, 0, 0))),
        compiler_params=_cparams(("parallel",)),
    )(ctx, mod4, ng, w_kv)


def _ret_kernel(q_ref, k_ref, v_ref, kc_ref, vc_ref, dm_ref, wt_ref, ob_ref, sg_ref, o_ref, s_ref, *, chunk):
    @pl.when(pl.program_id(1) == 0)
    def _():
        _ret_init_state(kc_ref, vc_ref, s_ref, False)

    state = [s_ref[h] for h in range(RET_HEADS)]
    staged = []
    for ci in range(q_ref.shape[1] // chunk):
        rows = slice(ci * chunk, (ci + 1) * chunk)
        for h in range(RET_HEADS):
            q = q_ref[0, rows, h * RET_DK:(h + 1) * RET_DK]
            k = k_ref[0, rows, h * RET_DK:(h + 1) * RET_DK]
            v = v_ref[0, rows, h * RET_DV:(h + 1) * RET_DV]
            sc = _dot_nt(q, k)
            rhs = jnp.concatenate([v, _mx(state[h])], axis=0)
            qd = _mx(_scale_rows(q.astype(F32), wt_ref[h, 1]))
            state[h] = (state[h] * math.exp(_ret_log_gamma(h, False) * chunk)
                        + _dot_tn(_scale_rows(k.astype(F32), wt_ref[h, 0]), v))
            staged.append((rows, h, sc, qd, rhs))
    for h in range(RET_HEADS):
        s_ref[h] = state[h]
    for rows, h, sc, qd, rhs in staged:
        vs = slice(h * RET_DV, (h + 1) * RET_DV)
        lhs = jnp.concatenate([_mx(sc * dm_ref[h]), qd], axis=1)
        o = jnp.dot(lhs, rhs, preferred_element_type=F32) + ob_ref[0, rows, vs].astype(F32)
        o = o * lax.rsqrt(_mean_rows(o * o) + EPS)
        o_ref[0, rows, vs] = (o * sg_ref[0, rows, vs].astype(F32)).astype(o_ref.dtype)


def _ret_decay_matrix(tb):
    rel = np.arange(tb)[:, None] - np.arange(tb)[None, :]
    out = np.zeros((RET_HEADS, tb, tb), np.float64)
    for h in range(RET_HEADS):
        fw = np.exp(_ret_log_gamma(h, False) * np.maximum(rel, 0))
        bw = np.exp(_ret_log_gamma(h, True) * np.maximum(-rel, 0))
        out[h] = np.where(rel > 0, fw, np.where(rel < 0, bw, 2.0))
    return jnp.asarray(out, F32)


def _ret_scan(q, k, v, kc, vc, o_bw, sg, *, tb, chunk):
    b, l, _ = q.shape
    lc = kc.shape[1]
    row = lambda w: pl.BlockSpec((1, tb, w), lambda i, j: (i, j, 0))
    ctx = lambda w: pl.BlockSpec((1, lc, w), lambda i, j: (i, 0, 0))
    return pl.pallas_call(
        functools.partial(_ret_kernel, chunk=chunk),
        out_shape=jax.ShapeDtypeStruct((b, l, ODD_V), MXU_DTYPE),
        grid=(b, l // tb),
        in_specs=[row(ODD_Q), row(ODD_Q), row(ODD_V), ctx(ODD_Q), ctx(ODD_V),
                  _const_spec((RET_HEADS, chunk, chunk)), _const_spec((RET_HEADS, 2, chunk, LANES)),
                  row(ODD_V), row(ODD_V)],
        out_specs=row(ODD_V),
        scratch_shapes=[pltpu.VMEM((RET_HEADS, RET_DK, RET_DV), F32)],
        compiler_params=_cparams(("parallel", "arbitrary")),
    )(q, k, v, kc, vc, _ret_decay_matrix(chunk), _ret_tables(chunk, False), o_bw, sg)


def _axial_tables(n_tok):
    t = np.arange(n_tok)
    n_freq = HEAD_DIM // 4
    inv = ROPE_BASE ** (-np.arange(n_freq, dtype=np.float64) / n_freq)
    ang = np.concatenate([(t // GRID_W)[:, None] * inv, (t % GRID_W)[:, None] * inv], axis=-1)
    cos, sin = np.cos(ang), np.sin(ang)
    cs = np.tile(np.concatenate([cos, cos], axis=-1), (1, LANES // HEAD_DIM))
    sn = np.tile(np.concatenate([-sin, sin], axis=-1), (1, LANES // HEAD_DIM))
    return jnp.asarray(cs, F32), jnp.asarray(sn, F32)


def _retention_tables(n_tok):
    theta = 1.0 / (RET_BASE ** np.linspace(0.0, 1.0, RET_DK // 2))
    ang = np.arange(n_tok, dtype=np.float64)[:, None] * theta
    return jnp.asarray(np.cos(ang), F32), jnp.asarray(np.sin(ang), F32)


def _block_diag_mean(width, head):
    idx = np.arange(width) // head
    return jnp.asarray((idx[:, None] == idx[None, :]) / head, MXU_DTYPE)


def _pick(n, pref):
    while n % pref:
        pref //= 2
    return pref


def kernel(x, c, ctx, c_ctx, mod_w, mod_b, norm_g, ffn_w_in, ffn_w_out, even_w_in, even_w_out,
           attn_qk_norm_g, attn_sink, hgrn_out_norm_g, hgrn_lb, odd_w_in, odd_w_out):
    n_b, n_tok, d = x.shape
    n_ctx = ctx.shape[1]
    depth = mod_w.shape[0]
    assert depth == 2 and d == D_MODEL

    rows = -(-(n_b + 1) // 8) * 8
    cond = jnp.zeros((rows, d), F32).at[:n_b].set(c).at[n_b].set(c_ctx)
    mod4 = _modulation(cond, mod_w, mod_b).reshape(depth, rows, 6, d)
    lat_row = lambda i: i
    ctx_row = lambda i: n_b

    ng1, ng2 = norm_g[0, 0][None, :], norm_g[0, 1][None, :]
    w_in = _mx(even_w_in[0])
    gq = jnp.tile(attn_qk_norm_g[0, 0], ATTN_HEADS)[None, :]
    gk = jnp.tile(attn_qk_norm_g[0, 1], ATTN_KV_HEADS)[None, :]
    bdq, bdk = _block_diag_mean(A_Q, HEAD_DIM), _block_diag_mean(A_KV, HEAD_DIM)
    cs_a, sn_a = _axial_tables(n_tok)
    lower = jnp.cumsum(jax.nn.softmax(hgrn_lb.astype(F32), axis=0), axis=0)[0][None, :]
    og = hgrn_out_norm_g[0][None, :]
    wo = _mx(even_w_out[0])
    wi, wd = _mx(ffn_w_in[0]), _mx(ffn_w_out[0])

    n_flat = n_b * n_ctx
    flat = lambda a: a.reshape(1, n_flat, a.shape[-1])
    unflat = lambda a: a.reshape(n_b, n_ctx, a.shape[-1])
    tm_c = _pick(n_flat, 1024)
    ones_c = jnp.ones((n_flat, LANES), F32)
    qc, kc, vc, hqc, hlc, hic, hgc = map(unflat, _proj_even(
        flat(ctx), mod4, 0, ctx_row, ng1, w_in, gq, gk, bdq, bdk, ones_c, ones_c, lower,
        rope=False, tm=tm_c, n_split=2))
    ql, kl, vl, hql, hll, hil, hgl = _proj_even(x, mod4, 0, lat_row, ng1, w_in, gq, gk, bdq, bdk, cs_a, sn_a,
                                                lower, rope=True, tm=_pick(n_tok, 1024), n_split=2)

    a_ctx = _attention(qc, None, None, kc, vc, attn_sink[0], window=False, tq=WINDOW)
    a_lat = _attention(ql, kl, vl, kc, vc, attn_sink[0], window=True, tq=_pick(n_tok, 8 * WINDOW))

    zeros = jnp.zeros((n_b, HGRN_HEADS, HGRN_DK, HGRN_DK), F32)
    sub_c, sub_l = _pick(n_ctx, 256), _pick(n_tok, 256)
    o_fw_c, o_bw_c, s_fw, s_bw = _hgrn_scan(hqc, hlc, hic, zeros, zeros, tb=sub_c, sub=sub_c)
    o_fw, o_bw, _, _ = _hgrn_scan(hql, hll, hil, s_fw, s_bw, tb=_pick(n_tok, 4 * sub_l), sub=sub_l)

    ctx1 = unflat(_out_layer([flat(a_ctx)], flat(ctx), mod4, 0, ctx_row, ng2, wo, wi, wd, tm=tm_c, n_split=2,
                             gated=(flat(o_fw_c), flat(o_bw_c), flat(hgc), og)))
    x1 = _out_layer([a_lat], x, mod4, 0, lat_row, ng2, wo, wi, wd, tm=_pick(n_tok, 1024), n_split=2,
                    gated=(o_fw, o_bw, hgl, og))

    ng1, ng2 = norm_g[1, 0][None, :], norm_g[1, 1][None, :]
    w_in = _mx(odd_w_in[0])
    cs_r, sn_r = _retention_tables(n_tok)
    wo = _mx(odd_w_out[0])
    wi, wd = _mx(ffn_w_in[1]), _mx(ffn_w_out[1])

    kc, vc = map(unflat, _proj_ctx_kv(ctx1.reshape(n_flat // tm_c, tm_c, d), mod4, 1, n_b, ng1,
                                      w_in[:, ODD_Q:2 * ODD_Q + ODD_V]))
    tb_r = _pick(n_tok, 256)
    q, k, v, sg, o_bw = _proj_odd(x1, mod4, 1, ng1, w_in, cs_r, sn_r, kc, vc, tm=_pick(n_tok, 512), chunk=tb_r)
    mix = _ret_scan(q, k, v, kc, vc, o_bw, sg, tb=_pick(n_tok, 512), chunk=tb_r)
    return _out_layer([mix], x1, mod4, 1, lat_row, ng2, wo, wi, wd, tm=_pick(n_tok, 1024), n_split=2)
```
